```python
import math
import jax, jax.numpy as jnp
from jax import lax
import numpy as np

D_MODEL = 1024
BATCH = 4
SEQ = 8192
DEPTH = 2

GRID_W = 64
CTX_LEN = 256

H_A = 8
KV_A = 2
G_A = H_A // KV_A
DH_A = 64
WINDOW = 128
BLOCK = 128
BAND = BLOCK + 2 * WINDOW

H_B = 8
NOPE_B = 64
ROPE_B = 32
V_B = 64
Q_RANK = 384
KV_RANK = 256
MLA_SCALE = (NOPE_B + ROPE_B) ** -0.5

W_QA = H_A * DH_A
W_KA = KV_A * DH_A
W_VA = KV_A * DH_A
IN_WIDTHS = (W_QA, W_KA, W_VA, Q_RANK, KV_RANK, ROPE_B)
IN_DIM = sum(IN_WIDTHS)
SPLIT_POINTS = tuple(int(v) for v in np.cumsum(IN_WIDTHS)[:-1])
MIX_DIM = H_A * DH_A + H_B * V_B

N_EXPERTS = 16
CAP_FACTOR = 2
D_FF = 512

ROPE_BASE = 10000.0
EPS = 1e-6

kernel_name = "hybrid_swa_mla_ec_moe_dit"


def rmsnorm(x, w):
    xf = x.astype(jnp.float32)
    y = xf * lax.rsqrt(jnp.mean(xf * xf, axis=-1, keepdims=True) + EPS)
    return (y * w.astype(jnp.float32)).astype(x.dtype)


def modulate(h, shift, scale):
    return h * (1 + scale) + shift


def axial_rope(x, rows, cols):
    d = x.shape[-1]
    da = d // 2
    nf = da // 2
    inv = ROPE_BASE ** (-jnp.arange(nf, dtype=jnp.float32) / nf)

    def rot(xa, pos):
        ang = pos.astype(jnp.float32)[:, None] * inv[None, :]
        cos = jnp.cos(ang)[None, :, None, :]
        sin = jnp.sin(ang)[None, :, None, :]
        x1, x2 = xa[..., :nf], xa[..., nf:]
        return jnp.concatenate([x1 * cos - x2 * sin, x1 * sin + x2 * cos], axis=-1)

    xf = x.astype(jnp.float32)
    out = jnp.concatenate([rot(xf[..., :da], rows), rot(xf[..., da:], cols)], axis=-1)
    return out.astype(x.dtype)


def gqa_sink_attend(qb, keys, vals, bias, sink_g):
    s = jnp.einsum('bqkgd,bskd->bkgqs', qb, keys, preferred_element_type=jnp.float32) * (DH_A ** -0.5)
    if bias is not None:
        s = s + bias
    b, q = qb.shape[0], qb.shape[1]
    s_sink = jnp.broadcast_to(sink_g.astype(jnp.float32)[None, :, :, None, None], (b, KV_A, G_A, q, 1))
    p = jax.nn.softmax(jnp.concatenate([s, s_sink], axis=-1), axis=-1)[..., :-1]
    o = jnp.einsum('bkgqs,bskd->bqkgd', p.astype(vals.dtype), vals)
    return o.reshape(b, q, H_A * DH_A)


def window_attention_latent(q, k, v, kc, vc, sink_g):
    b, s_len = q.shape[:2]
    lc = kc.shape[1]
    nb = s_len // BLOCK
    qg = q.reshape(b, s_len, KV_A, G_A, DH_A)
    pad = ((0, 0), (WINDOW, WINDOW), (0, 0), (0, 0))
    kp = jnp.pad(k, pad)
    vp = jnp.pad(v, pad)
    ctx_bias = jnp.zeros((BLOCK, lc), jnp.float32)

    def block(i):
        start = i * BLOCK
        qb = lax.dynamic_slice_in_dim(qg, start, BLOCK, axis=1)
        kb = lax.dynamic_slice_in_dim(kp, start, BAND, axis=1)
        vb = lax.dynamic_slice_in_dim(vp, start, BAND, axis=1)
        tq = start + jnp.arange(BLOCK)
        ts = start - WINDOW + jnp.arange(BAND)
        valid = (jnp.abs(tq[:, None] - ts[None, :]) <= WINDOW) & (ts[None, :] >= 0) & (ts[None, :] < s_len)
        band_bias = jnp.where(valid, 0.0, -jnp.inf).astype(jnp.float32)
        keys = jnp.concatenate([kc, kb], axis=1)
        vals = jnp.concatenate([vc, vb], axis=1)
        bias = jnp.concatenate([ctx_bias, band_bias], axis=1)
        return gqa_sink_attend(qb, keys, vals, bias, sink_g)

    out = lax.map(block, jnp.arange(nb))
    return jnp.transpose(out, (1, 0, 2, 3)).reshape(b, s_len, H_A * DH_A)


def mla_attend(qn, qr, kn, kr, v):
    s = (jnp.einsum('bqhd,bkhd->bhqk', qn, kn, preferred_element_type=jnp.float32)
         + jnp.einsum('bqhr,bkr->bhqk', qr, kr, preferred_element_type=jnp.float32)) * MLA_SCALE
    p = jax.nn.softmax(s, axis=-1)
    o = jnp.einsum('bhqk,bkhd->bqhd', p.astype(v.dtype), v)
    return o.reshape(qn.shape[0], qn.shape[1], H_B * V_B)


def mla_latent(qn, qr, kn, kr, v):
    b, s_len = qn.shape[:2]
    nb = s_len // BLOCK

    def block(i):
        start = i * BLOCK
        qnb = lax.dynamic_slice_in_dim(qn, start, BLOCK, axis=1)
        qrb = lax.dynamic_slice_in_dim(qr, start, BLOCK, axis=1)
        return mla_attend(qnb, qrb, kn, kr, v)

    out = lax.map(block, jnp.arange(nb))
    return jnp.transpose(out, (1, 0, 2, 3)).reshape(b, s_len, H_B * V_B)


def project_groups(t, w_in, q_norm, kv_norm, w_q_up, w_kv_up):
    bt, n = t.shape[:2]
    p = jnp.einsum('bsd,de->bse', t, w_in)
    qa, ka, va, cq, ckv, kr = jnp.split(p, SPLIT_POINTS, axis=-1)
    qa = qa.reshape(bt, n, H_A, DH_A)
    ka = ka.reshape(bt, n, KV_A, DH_A)
    va = va.reshape(bt, n, KV_A, DH_A)
    qb = jnp.einsum('bsr,re->bse', rmsnorm(cq, q_norm), w_q_up).reshape(bt, n, H_B, NOPE_B + ROPE_B)
    kvb = jnp.einsum('bsr,re->bse', rmsnorm(ckv, kv_norm), w_kv_up).reshape(bt, n, H_B, NOPE_B + V_B)
    return (qa, ka, va, qb[..., :NOPE_B], qb[..., NOPE_B:], kr,
            kvb[..., :NOPE_B], kvb[..., NOPE_B:])


def token_mixers(h, hc, rows, cols, w_in, sink, q_norm, kv_norm, w_q_up, w_kv_up, w_out, with_ctx_out):
    sink_g = sink.reshape(KV_A, G_A)
    qa, ka, va, qbn, qbr, kr, kbn, vb = project_groups(h, w_in, q_norm, kv_norm, w_q_up, w_kv_up)
    qa_c, ka_c, va_c, qbn_c, qbr_c, kr_c, kbn_c, vb_c = project_groups(hc, w_in, q_norm, kv_norm, w_q_up, w_kv_up)
    qa = axial_rope(qa, rows, cols)
    ka = axial_rope(ka, rows, cols)
    qbr = axial_rope(qbr, rows, cols)
    kr = axial_rope(kr[:, :, None, :], rows, cols)[:, :, 0, :]
    o_a = window_attention_latent(qa, ka, va, ka_c, va_c, sink_g)
    o_b = mla_latent(qbn, qbr,
                     jnp.concatenate([kbn_c, kbn], axis=1),
                     jnp.concatenate([kr_c, kr], axis=1),
                     jnp.concatenate([vb_c, vb], axis=1))
    out = jnp.einsum('bse,ed->bsd', jnp.concatenate([o_a, o_b], axis=-1), w_out)
    if not with_ctx_out:
        return out, None
    lc = hc.shape[1]
    o_a_c = gqa_sink_attend(qa_c.reshape(hc.shape[0], lc, KV_A, G_A, DH_A), ka_c, va_c, None, sink_g)
    o_b_c = mla_attend(qbn_c, qbr_c, kbn_c, kr_c, vb_c)
    out_c = jnp.einsum('bse,ed->bsd', jnp.concatenate([o_a_c, o_b_c], axis=-1), w_out)
    return out, out_c


def ec_moe(h, w_router, w_gate, w_up, w_down):
    n = h.shape[1]
    cap = CAP_FACTOR * n // N_EXPERTS
    aff = jax.nn.softmax(jnp.einsum('bnd,de->bne', h, w_router, preferred_element_type=jnp.float32), axis=-1)
    vals, idx = lax.top_k(jnp.swapaxes(aff, 1, 2), cap)

    def per_sample(hs, idx_s, val_s):
        xs = hs[idx_s]
        a = jnp.einsum('ecd,edf->ecf', xs, w_gate)
        u = jnp.einsum('ecd,edf->ecf', xs, w_up)
        y = jnp.einsum('ecf,efd->ecd', jax.nn.silu(a) * u, w_down) * val_s[..., None].astype(hs.dtype)
        return jnp.zeros_like(hs).at[idx_s.reshape(-1)].add(y.reshape(-1, hs.shape[-1]).astype(hs.dtype))

    return jax.vmap(per_sample)(h, idx, vals)


def setup_inputs(seed: int = 0) -> dict:
    key = jax.random.key(seed)
    ks = jax.random.split(key, 20)
    f32 = jnp.float32
    L = DEPTH

    def nrm(k, shape, scale):
        return jax.random.normal(k, shape, f32) * scale

    return {
        "x": nrm(ks[0], (BATCH, SEQ, D_MODEL), 1.0),
        "c": nrm(ks[1], (BATCH, D_MODEL), 1.0),
        "ctx": nrm(ks[2], (BATCH, CTX_LEN, D_MODEL), 1.0),
        "c_ctx": nrm(ks[3], (D_MODEL,), 1.0),
        "w_mod": nrm(ks[4], (L, D_MODEL, 6 * D_MODEL), 0.5 * D_MODEL ** -0.5),
        "b_mod": nrm(ks[5], (L, 6 * D_MODEL), 0.01),
        "norm_attn": 1.0 + nrm(ks[6], (L, D_MODEL), 0.02),
        "norm_ffn": 1.0 + nrm(ks[7], (L, D_MODEL), 0.02),
        "w_in": nrm(ks[8], (L, D_MODEL, IN_DIM), D_MODEL ** -0.5),
        "sink": nrm(ks[9], (L, H_A), 0.5),
        "q_norm": 1.0 + nrm(ks[10], (L, Q_RANK), 0.02),
        "kv_norm": 1.0 + nrm(ks[11], (L, KV_RANK), 0.02),
        "w_q_up": nrm(ks[12], (L, Q_RANK, H_B * (NOPE_B + ROPE_B)), Q_RANK ** -0.5),
        "w_kv_up": nrm(ks[13], (L, KV_RANK, H_B * (NOPE_B + V_B)), KV_RANK ** -0.5),
        "w_out": nrm(ks[14], (L, MIX_DIM, D_MODEL), MIX_DIM ** -0.5),
        "w_router": nrm(ks[15], (L, D_MODEL, N_EXPERTS), D_MODEL ** -0.5),
        "w_gate": nrm(ks[16], (L, N_EXPERTS, D_MODEL, D_FF), D_MODEL ** -0.5),
        "w_up": nrm(ks[17], (L, N_EXPERTS, D_MODEL, D_FF), D_MODEL ** -0.5),
        "w_down": nrm(ks[18], (L, N_EXPERTS, D_FF, D_MODEL), D_FF ** -0.5),
        "norm_final": 1.0 + nrm(ks[19], (D_MODEL,), 0.02),
    }


def reference(x, c, ctx, c_ctx, w_mod, b_mod, norm_attn, norm_ffn, w_in, sink, q_norm, kv_norm,
              w_q_up, w_kv_up, w_out, w_router, w_gate, w_up, w_down, norm_final):
    n_lat = x.shape[1]
    ROWS = n_lat // GRID_W
    rows = jnp.repeat(jnp.arange(ROWS, dtype=jnp.int32), GRID_W)
    cols = jnp.tile(jnp.arange(GRID_W, dtype=jnp.int32), ROWS)
    xc = ctx
    for l in range(DEPTH):
        last = l == DEPTH - 1
        mod = jnp.einsum('bd,de->be', jax.nn.silu(c), w_mod[l]) + b_mod[l]
        mod_c = jnp.einsum('d,de->e', jax.nn.silu(c_ctx), w_mod[l]) + b_mod[l]
        sh_a, sc_a, g_a, sh_f, sc_f, g_f = [m[:, None, :] for m in jnp.split(mod, 6, axis=-1)]
        csh_a, csc_a, cg_a, csh_f, csc_f, cg_f = jnp.split(mod_c, 6, axis=-1)

        h = modulate(rmsnorm(x, norm_attn[l]), sh_a, sc_a)
        hc = modulate(rmsnorm(xc, norm_attn[l]), csh_a, csc_a)
        mix, mix_c = token_mixers(h, hc, rows, cols, w_in[l], sink[l], q_norm[l], kv_norm[l],
                                  w_q_up[l], w_kv_up[l], w_out[l], not last)
        x = x + g_a * mix
        h2 = modulate(rmsnorm(x, norm_ffn[l]), sh_f, sc_f)
        x = x + g_f * ec_moe(h2, w_router[l], w_gate[l], w_up[l], w_down[l])

        if not last:
            xc = xc + cg_a * mix_c
            h2c = modulate(rmsnorm(xc, norm_ffn[l]), csh_f, csc_f)
            xc = xc + cg_f * ec_moe(h2c, w_router[l], w_gate[l], w_up[l], w_down[l])
    return rmsnorm(x, norm_final)
```

```python
import functools
import math

import jax
import jax.numpy as jnp
import numpy as np
from jax import lax
from jax.experimental import pallas as pl
from jax.experimental.pallas import tpu as pltpu

F32 = jnp.float32
BF16 = jnp.bfloat16

GRID_W = 64
H_A, KV_A, DH_A = 8, 2, 64
G_A = H_A // KV_A
WINDOW = 128
H_B, NOPE_B, ROPE_B, V_B = 8, 64, 32, 64
Q_RANK, KV_RANK = 384, 256
N_EXPERTS, CAP_FACTOR, D_FF = 16, 2, 512
ROPE_BASE = 10000.0
EPS = 1e-6
LOG2E = math.log2(math.e)
LANES = 128
SUBLANES = 8
NEG_INF = float("-inf")

C_QA, C_KA, C_VA, C_CQ, C_CKV, C_KR, C_END = 0, 512, 640, 768, 1152, 1408, 1536
VMEM_LIMIT = 56 * 1024 * 1024


def _cparams(sem):
    return pltpu.CompilerParams(dimension_semantics=sem, vmem_limit_bytes=VMEM_LIMIT)


def _nt(a, b):
    return lax.dot_general(a, b, (((1,), (1,)), ((), ())), preferred_element_type=F32)


def _dot(a, b):
    return jnp.dot(a, b, preferred_element_type=F32)


def _split(a):
    hi = a.astype(BF16)
    lo = (a - hi.astype(F32)).astype(BF16)
    return hi, lo


def _dot3(a, b_hi, b_lo):
    a_hi, a_lo = _split(a)
    return _dot(a_hi, b_hi) + (_dot(a_hi, b_lo) + _dot(a_lo, b_hi))


def _lane_tile(t, reps):
    return t if reps == 1 else jnp.concatenate([t] * reps, axis=1)


def _rope(x, cos, sin, half):
    w = x.shape[1]
    reps = w // LANES
    lane = lax.broadcasted_iota(jnp.int32, x.shape, 1)
    first = (lane & (2 * half - 1)) < half
    partner = jnp.where(first, pltpu.roll(x, w - half, 1), pltpu.roll(x, half, 1))
    return x * _lane_tile(cos, reps) + partner * _lane_tile(sin, reps)


def _row_to_col(v):
    n = v.shape[1]
    eye = lax.broadcasted_iota(jnp.int32, (n, n), 0) == lax.broadcasted_iota(jnp.int32, (n, n), 1)
    diag = jnp.where(eye, jnp.broadcast_to(v, (n, n)), 0.0)
    ones = jnp.ones((n, LANES), BF16)
    hi = diag.astype(BF16)
    r1 = diag - hi.astype(F32)
    mid = r1.astype(BF16)
    lo = (r1 - mid.astype(F32)).astype(BF16)
    return _dot(hi, ones) + (_dot(mid, ones) + _dot(lo, ones))


def _rms(x, w):
    ms = jnp.mean(x * x, axis=-1, keepdims=True)
    return x * lax.rsqrt(ms + EPS) * w


def _mod_body(c_ref, w_ref, b_ref, o_ref):
    c = c_ref[...]
    s = c * (1.0 / (1.0 + jnp.exp(-c)))
    w_hi, w_lo = _split(w_ref[0])
    o_ref[0] = _dot3(s, w_hi, w_lo) + b_ref[0]


def _mod_call(cs, w_mod, b_mod):
    depth, d, d6 = w_mod.shape
    tn = 1536
    return pl.pallas_call(
        _mod_body,
        grid=(depth, d6 // tn),
        in_specs=[
            pl.BlockSpec((SUBLANES, d), lambda l, j: (0, 0)),
            pl.BlockSpec((1, d, tn), lambda l, j: (l, 0, j)),
            pl.BlockSpec((1, 1, tn), lambda l, j: (l, 0, j)),
        ],
        out_specs=pl.BlockSpec((1, SUBLANES, tn), lambda l, j: (l, 0, j)),
        out_shape=jax.ShapeDtypeStruct((depth, SUBLANES, d6), F32),
        compiler_params=_cparams(("arbitrary", "arbitrary")),
        name="mod",
    )(cs, w_mod, b_mod.reshape(depth, 1, d6))


def _proj_body(x_ref, mod_ref, nw_ref, win_ref, qn_ref, kvn_ref, wq_ref, wkv_ref,
               caq, saq, cak, sak, cbq, sbq, cbk, sbk,
               qa_o, ka_o, va_o, qb_o, kb_o, vb_o):
    d = x_ref.shape[1]
    m = mod_ref[0]
    y = _rms(x_ref[...], nw_ref[...])
    h = (y * (1.0 + m[:, d:2 * d]) + m[:, 0:d]).astype(BF16)
    p = _dot(h, win_ref[...])
    qa_o[...] = _rope(p[:, C_QA:C_KA], caq[...], saq[...], 16).astype(BF16)
    ka_o[...] = _rope(p[:, C_KA:C_VA], cak[...], sak[...], 16).astype(BF16)
    va_o[...] = p[:, C_VA:C_CQ].astype(BF16)
    cq = _rms(p[:, C_CQ:C_CKV], qn_ref[...]).astype(BF16)
    qb = _dot(cq, wq_ref[...])
    qb_o[...] = _rope(qb, cbq[...], sbq[...], 8).astype(BF16)
    ckv = _rms(p[:, C_CKV:C_KR], kvn_ref[...]).astype(BF16)
    kvb = _dot(ckv, wkv_ref[...])
    kr = _rope(p[:, C_KR:C_END], cbk[...], sbk[...], 8)
    kw = H_B * LANES
    kb_o[...] = (kvb[:, 0:kw] + _lane_tile(kr, H_B)).astype(BF16)
    vb_o[...] = kvb[:, kw:].astype(BF16)


def _proj_call(x2, mod3, mod_row_fn, nw, win, qn, kvn, wq, wkv, tabs, tm, tab_blocks):
    t, d = x2.shape
    full = lambda a: pl.BlockSpec(a.shape, lambda i: (0,) * a.ndim)
    tab_spec = pl.BlockSpec((tm, LANES), lambda i: (i % tab_blocks, 0))
    widths = (512, 128, 128, 1024, 1024, 512)
    return pl.pallas_call(
        _proj_body,
        grid=(t // tm,),
        in_specs=[
            pl.BlockSpec((tm, d), lambda i: (i, 0)),
            pl.BlockSpec((1, 1, mod3.shape[2]), lambda i: (mod_row_fn(i), 0, 0)),
            full(nw), full(win), full(qn), full(kvn), full(wq), full(wkv),
        ] + [tab_spec] * 8,
        out_specs=[pl.BlockSpec((tm, w), lambda i: (i, 0)) for w in widths],
        out_shape=[jax.ShapeDtypeStruct((t, w), BF16) for w in widths],
        compiler_params=_cparams(("arbitrary",)),
        name="proj",
    )(x2, mod3, nw, win, qn, kvn, wq, wkv, *tabs)


def _win_body(sink_ref, q_ref, kc_ref, vc_ref, *rest, n, tq, band):
    if band:
        k_ref, v_ref, o_ref = rest
    else:
        (o_ref,) = rest
    i = pl.program_id(1)
    kc = kc_ref[...]
    vc = vc_ref[...]
    if band:
        wb = tq + 2 * WINDOW
        start = jnp.clip(i * tq - WINDOW, 0, n - wb)
        start = pl.multiple_of(start, LANES)
        kb = k_ref[pl.ds(start, wb), :]
        vb = v_ref[pl.ds(start, wb), :]
        qpos = i * tq + lax.broadcasted_iota(jnp.int32, (tq, wb), 0)
        kpos = start + lax.broadcasted_iota(jnp.int32, (tq, wb), 1)
        valid = jnp.abs(qpos - kpos) <= WINDOW
    lo = lax.broadcasted_iota(jnp.int32, (tq, LANES), 1) < DH_A
    zero = jnp.zeros((tq, LANES), BF16)
    o_g = []
    for g in range(KV_A):
        sel = lo if g == 0 else jnp.logical_not(lo)
        qg = jnp.concatenate(
            [jnp.where(sel, q_ref[:, p * LANES:(p + 1) * LANES], zero) for p in range(G_A)], axis=0)
        s_c = _nt(qg, kc)
        if band:
            s_b = _nt(qg, kb)
        o_p = []
        for p in range(G_A):
            h = p + g * G_A
            sink2 = sink_ref[h] * LOG2E
            sc = s_c[p * tq:(p + 1) * tq]
            m = jnp.maximum(jnp.max(sc, axis=-1, keepdims=True), sink2)
            if band:
                sb = jnp.where(valid, s_b[p * tq:(p + 1) * tq], NEG_INF)
                m = jnp.maximum(m, jnp.max(sb, axis=-1, keepdims=True))
            pc = jnp.exp2(sc - m)
            den = jnp.sum(pc, axis=-1, keepdims=True) + jnp.exp2(sink2 - m)
            o = _dot(pc.astype(BF16), vc)
            if band:
                pb = jnp.exp2(sb - m)
                den = den + jnp.sum(pb, axis=-1, keepdims=True)
                o = o + _dot(pb.astype(BF16), vb)
            o_p.append(o * (1.0 / den))
        o_g.append(o_p)
    for p in range(G_A):
        o_ref[:, p * LANES:(p + 1) * LANES] = jnp.where(lo, o_g[0][p], o_g[1][p]).astype(BF16)


def _win_call(sink, q, kc, vc, k, v, bsz, n, lc, tq):
    band = k is not None
    nq = n // tq
    in_specs = [
        pl.BlockSpec(memory_space=pltpu.SMEM),
        pl.BlockSpec((tq, H_A * DH_A), lambda b, i: (b * nq + i, 0)),
        pl.BlockSpec((lc, LANES), lambda b, i: (b, 0)),
        pl.BlockSpec((lc, LANES), lambda b, i: (b, 0)),
    ]
    args = [sink, q, kc, vc]
    if band:
        in_specs += [pl.BlockSpec((n, LANES), lambda b, i: (b, 0))] * 2
        args += [k, v]
    return pl.pallas_call(
        functools.partial(_win_body, n=n, tq=tq, band=band),
        grid=(bsz, nq),
        in_specs=in_specs,
        out_specs=pl.BlockSpec((tq, H_A * DH_A), lambda b, i: (b * nq + i, 0)),
        out_shape=jax.ShapeDtypeStruct((bsz * n, H_A * DH_A), BF16),
        compiler_params=_cparams(("arbitrary", "arbitrary")),
        name="win_lat" if band else "win_ctx",
    )(*args)


def _mla_body(q_ref, kc_ref, vc_ref, *rest, n, tk, has_lat):
    if has_lat:
        k_ref, v_ref, o_ref, m_sc, l_sc, acc_sc = rest
    else:
        (o_ref,) = rest
    tq = q_ref.shape[0]
    lo = lax.broadcasted_iota(jnp.int32, (tq, LANES), 1) < V_B

    def lane_sum(p):
        parts = [p[:, c * LANES:(c + 1) * LANES] for c in range(p.shape[1] // LANES)]
        while len(parts) > 1:
            parts = [a + b for a, b in zip(parts[0::2], parts[1::2])] + (
                [parts[-1]] if len(parts) % 2 else [])
        return parts[0]

    def lane_max(p):
        parts = [p[:, c * LANES:(c + 1) * LANES] for c in range(p.shape[1] // LANES)]
        while len(parts) > 1:
            parts = [jnp.maximum(a, b) for a, b in zip(parts[0::2], parts[1::2])] + (
                [parts[-1]] if len(parts) % 2 else [])
        return parts[0]

    def row_max_rep(p):
        r = jnp.max(lane_max(p), axis=-1, keepdims=True)
        return jnp.broadcast_to(r, (tq, LANES))

    qs, ms, ls, accs = [], [], [], []
    for hh in range(2):
        q = q_ref[:, hh * LANES:(hh + 1) * LANES]
        s = _nt(q, kc_ref[:, hh * LANES:(hh + 1) * LANES])
        m = row_max_rep(s)
        p = jnp.exp2(s - _lane_tile(m, s.shape[1] // LANES))
        qs.append(q)
        ms.append(m)
        ls.append(lane_sum(p))
        accs.append(_dot(p.astype(BF16), vc_ref[...]))

    if has_lat:
        for hh in range(2):
            m_sc[hh] = ms[hh]
            l_sc[hh] = ls[hh]
            acc_sc[hh] = accs[hh]

        def chunk(c, carry):
            off = pl.multiple_of(c * tk, tk)
            vch = v_ref[pl.ds(off, tk), :]
            for hh in range(2):
                kch = k_ref[pl.ds(off, tk), hh * LANES:(hh + 1) * LANES]
                s = _nt(qs[hh], kch)
                m_old = m_sc[hh]
                m_new = jnp.maximum(m_old, row_max_rep(s))
                alpha = jnp.exp2(m_old - m_new)
                p = jnp.exp2(s - _lane_tile(m_new, tk // LANES))
                m_sc[hh] = m_new
                l_sc[hh] = alpha * l_sc[hh] + lane_sum(p)
                acc_sc[hh] = alpha * acc_sc[hh] + _dot(p.astype(BF16), vch)
            return carry

        lax.fori_loop(0, n // tk, chunk, 0)
        ls = [l_sc[0], l_sc[1]]
        accs = [acc_sc[0], acc_sc[1]]

    outs = [accs[hh] * (1.0 / jnp.sum(ls[hh], axis=-1, keepdims=True)) for hh in range(2)]
    o_ref[...] = jnp.where(lo, outs[0], outs[1]).astype(BF16)


def _mla_call(q, kc, vc, k, v, bsz, n, lc, tq, tk):
    has_lat = k is not None
    nq = n // tq
    npair = H_B // 2
    in_specs = [
        pl.BlockSpec((tq, 2 * LANES), lambda b, h, i: (b * nq + i, h)),
        pl.BlockSpec((lc, 2 * LANES), lambda b, h, i: (b, h)),
        pl.BlockSpec((lc, LANES), lambda b, h, i: (b, h)),
    ]
    args = [q, kc, vc]
    scratch = []
    if has_lat:
        in_specs += [
            pl.BlockSpec((n, 2 * LANES), lambda b, h, i: (b, h)),
            pl.BlockSpec((n, LANES), lambda b, h, i: (b, h)),
        ]
        args += [k, v]
        scratch = [pltpu.VMEM((2, tq, LANES), F32)] * 3
    return pl.pallas_call(
        functools.partial(_mla_body, n=n, tk=tk, has_lat=has_lat),
        grid=(bsz, npair, nq),
        in_specs=in_specs,
        out_specs=pl.BlockSpec((tq, LANES), lambda b, h, i: (b * nq + i, h)),
        out_shape=jax.ShapeDtypeStruct((bsz * n, H_B * V_B), BF16),
        scratch_shapes=scratch,
        compiler_params=_cparams(("arbitrary", "arbitrary", "arbitrary")),
        name="mla_lat" if has_lat else "mla_ctx",
    )(*args)


def _outp_body(oa_ref, ob_ref, x_ref, mod_ref, wa_ref, wb_ref, nw_ref, rhi_ref, rlo_ref,
               x_o, h_o, aff_o):
    d = x_ref.shape[1]
    tm = x_ref.shape[0]
    m = mod_ref[0]
    mix = _dot(oa_ref[...], wa_ref[...]) + _dot(ob_ref[...], wb_ref[...])
    x = x_ref[...] + m[:, 2 * d:3 * d] * mix
    x_o[...] = x
    h = _rms(x, nw_ref[...]) * (1.0 + m[:, 4 * d:5 * d]) + m[:, 3 * d:4 * d]
    for s in range(d // LANES):
        h_o[pl.ds(s, tm, stride=SUBLANES), :] = h[:, s * LANES:(s + 1) * LANES]
    logits = _dot3(h, rhi_ref[...], rlo_ref[...])
    lane = lax.broadcasted_iota(jnp.int32, logits.shape, 1)
    logits = jnp.where(lane < N_EXPERTS, logits, NEG_INF)
    e = jnp.exp(logits - jnp.max(logits, axis=-1, keepdims=True))
    aff_o[...] = e / jnp.sum(e, axis=-1, keepdims=True)


def _outp_call(oa, ob, x2, mod3, mod_row_fn, wa, wb, nw, rhi, rlo, tm):
    t, d = x2.shape
    full = lambda a: pl.BlockSpec(a.shape, lambda i: (0,) * a.ndim)
    row = lambda w: pl.BlockSpec((tm, w), lambda i: (i, 0))
    return pl.pallas_call(
        _outp_body,
        grid=(t // tm,),
        in_specs=[row(oa.shape[1]), row(ob.shape[1]), row(d),
                  pl.BlockSpec((1, 1, mod3.shape[2]), lambda i: (mod_row_fn(i), 0, 0)),
                  full(wa), full(wb), full(nw), full(rhi), full(rlo)],
        out_specs=[row(d), pl.BlockSpec((tm * SUBLANES, LANES), lambda i: (i, 0)), row(LANES)],
        out_shape=[jax.ShapeDtypeStruct((t, d), F32),
                   jax.ShapeDtypeStruct((t * SUBLANES, LANES), F32),
                   jax.ShapeDtypeStruct((t, LANES), F32)],
        compiler_params=_cparams(("arbitrary",)),
        name="outp",
    )(oa, ob, x2, mod3, wa, wb, nw, rhi, rlo)


def _moe_body(idx_cur, idx_nxt, val_ref, h_hbm, wgu_ref, wd_ref, y_ref, buf, sem, *, ts, nt):
    be = pl.program_id(0)
    t = pl.program_id(1)
    step = be * nt + t
    total = pl.num_programs(0) * nt
    slot = step % 2

    def gather(idx_ref, tile, dst_slot):
        def one(j, carry):
            r = pl.multiple_of(idx_ref[0, 0, tile * ts + j] * SUBLANES, SUBLANES)
            pltpu.make_async_copy(
                h_hbm.at[pl.ds(r, SUBLANES), :],
                buf.at[dst_slot, pl.ds(pl.multiple_of(j * SUBLANES, SUBLANES), SUBLANES), :],
                sem.at[dst_slot]).start()
            return carry
        lax.fori_loop(0, ts, one, 0)

    @pl.when(step == 0)
    def _():
        gather(idx_cur, t, slot)

    @pl.when(jnp.logical_and(step + 1 < total, t + 1 < nt))
    def _():
        gather(idx_cur, t + 1, 1 - slot)

    @pl.when(jnp.logical_and(step + 1 < total, t + 1 == nt))
    def _():
        gather(idx_nxt, 0, 1 - slot)

    pltpu.make_async_copy(h_hbm.at[pl.ds(0, ts * SUBLANES), :], buf.at[slot], sem.at[slot]).wait()

    d = wgu_ref.shape[1]
    x = jnp.concatenate(
        [buf[slot, pl.ds(s, ts, stride=SUBLANES), :] for s in range(d // LANES)], axis=1).astype(BF16)
    au = _dot(x, wgu_ref[0])
    dff = au.shape[1] // 2
    a = au[:, :dff]
    hmid = (a * (1.0 / (1.0 + jnp.exp(-a))) * au[:, dff:]).astype(BF16)
    y = _dot(hmid, wd_ref[0])
    vrow = val_ref[0] if nt == 1 else val_ref[0, :, pl.ds(pl.multiple_of(t * ts, ts), ts)]
    y_ref[...] = y * _lane_tile(_row_to_col(vrow), d // LANES)


def _moe_call(idx3, val3, h8, wgu, wd, ts):
    nbe, _, cap = idx3.shape
    nt = cap // ts
    e = wgu.shape[0]
    d = wgu.shape[1]
    return pl.pallas_call(
        functools.partial(_moe_body, ts=ts, nt=nt),
        grid=(nbe, nt),
        in_specs=[
            pl.BlockSpec((1, 1, cap), lambda be, t: (be, 0, 0), memory_space=pltpu.SMEM),
            pl.BlockSpec((1, 1, cap), lambda be, t: (jnp.minimum(be + 1, nbe - 1), 0, 0),
                         memory_space=pltpu.SMEM),
            pl.BlockSpec((1, 1, cap), lambda be, t: (be, 0, 0)),
            pl.BlockSpec(memory_space=pl.ANY),
            pl.BlockSpec((1,) + wgu.shape[1:], lambda be, t: (be % e, 0, 0)),
            pl.BlockSpec((1,) + wd.shape[1:], lambda be, t: (be % e, 0, 0)),
        ],
        out_specs=pl.BlockSpec((ts, d), lambda be, t: (be * nt + t, 0)),
        out_shape=jax.ShapeDtypeStruct((nbe * cap, d), F32),
        scratch_shapes=[pltpu.VMEM((2, ts * SUBLANES, LANES), F32), pltpu.SemaphoreType.DMA((2,))],
        compiler_params=_cparams(("arbitrary", "arbitrary")),
        name="experts",
    )(idx3, idx3, val3, h8, wgu, wd)


def _resid_body(x_ref, y_ref, mod_ref, nw_ref, o_ref, *, final):
    d = x_ref.shape[1]
    x = x_ref[...] + mod_ref[0][:, 5 * d:6 * d] * y_ref[...]
    o_ref[...] = _rms(x, nw_ref[...]) if final else x


def _resid_call(x2, y2, mod3, mod_row_fn, nw, tm, final):
    t, d = x2.shape
    row = pl.BlockSpec((tm, d), lambda i: (i, 0))
    return pl.pallas_call(
        functools.partial(_resid_body, final=final),
        grid=(t // tm,),
        in_specs=[row, row, pl.BlockSpec((1, 1, mod3.shape[2]), lambda i: (mod_row_fn(i), 0, 0)),
                  pl.BlockSpec(nw.shape, lambda i: (0, 0))],
        out_specs=row,
        out_shape=jax.ShapeDtypeStruct((t, d), F32),
        compiler_params=_cparams(("arbitrary",)),
        name="resid",
    )(x2, y2, mod3, nw)


def _rope_tables(n):
    pos_r = (np.arange(n) // GRID_W).astype(np.float32)
    pos_c = (np.arange(n) % GRID_W).astype(np.float32)

    def pattern(nf):
        inv = ROPE_BASE ** (-np.arange(nf, dtype=np.float32) / nf)
        ar = pos_r[:, None] * inv[None, :]
        ac = pos_c[:, None] * inv[None, :]
        cos = np.concatenate([np.cos(ar), np.cos(ar), np.cos(ac), np.cos(ac)], axis=1)
        sin = np.concatenate([-np.sin(ar), np.sin(ar), -np.sin(ac), np.sin(ac)], axis=1)
        return cos.astype(np.float32), sin.astype(np.float32)

    ca, sa = pattern(DH_A // 4)
    cb, sb = pattern(ROPE_B // 4)
    ca2, sa2 = np.tile(ca, (1, 2)), np.tile(sa, (1, 2))
    one64 = np.ones((n, NOPE_B), np.float32)
    z64 = np.zeros((n, NOPE_B), np.float32)
    z32 = np.zeros((n, LANES - NOPE_B - ROPE_B), np.float32)
    cbp = np.concatenate([one64, cb, z32], axis=1)
    sbp = np.concatenate([z64, sb, z32], axis=1)
    sc_a = DH_A ** -0.5 * LOG2E
    sc_b = (NOPE_B + ROPE_B) ** -0.5 * LOG2E
    tabs = [ca2 * sc_a, sa2 * sc_a, ca2, sa2, cbp * sc_b, sbp * sc_b, cbp, sbp]
    return [jnp.asarray(t, F32) for t in tabs]


def _ident_tables(n):
    one = np.ones((n, LANES), np.float32)
    zero = np.zeros((n, LANES), np.float32)
    z32 = np.zeros((n, LANES - NOPE_B - ROPE_B), np.float32)
    onep = np.concatenate([np.ones((n, NOPE_B + ROPE_B), np.float32), z32], axis=1)
    sc_a = DH_A ** -0.5 * LOG2E
    sc_b = (NOPE_B + ROPE_B) ** -0.5 * LOG2E
    tabs = [one * sc_a, zero, one, zero, onep * sc_b, zero, onep, zero]
    return [jnp.asarray(t, F32) for t in tabs]


def _pair_perm():
    cols = []
    for p in range(G_A):
        for h in (p, G_A + p):
            cols.extend(range(h * DH_A, (h + 1) * DH_A))
    return np.asarray(cols)


def _prep_layer(w_in, w_q_up, w_kv_up, w_out, w_router, w_gate, w_up, w_down):
    d = w_in.shape[0]
    perm = _pair_perm()
    o_ka = H_A * DH_A
    o_va = o_ka + KV_A * DH_A
    o_cq = o_va + KV_A * DH_A
    o_ckv = o_cq + Q_RANK
    o_kr = o_ckv + KV_RANK
    z = lambda w: jnp.zeros((d, w), F32)
    win = jnp.concatenate([
        w_in[:, :o_ka][:, perm], w_in[:, o_ka:o_va], w_in[:, o_va:o_cq],
        w_in[:, o_cq:o_ckv], w_in[:, o_ckv:o_kr],
        z(NOPE_B), w_in[:, o_kr:], z(LANES - NOPE_B - ROPE_B)], axis=1).astype(BF16)
    wq = w_q_up.reshape(Q_RANK, H_B, NOPE_B + ROPE_B)
    wq = jnp.pad(wq, ((0, 0), (0, 0), (0, LANES - NOPE_B - ROPE_B))).reshape(Q_RANK, H_B * LANES)
    wkv = w_kv_up.reshape(KV_RANK, H_B, NOPE_B + V_B)
    wk = jnp.pad(wkv[:, :, :NOPE_B], ((0, 0), (0, 0), (0, LANES - NOPE_B))).reshape(KV_RANK, H_B * LANES)
    wv = wkv[:, :, NOPE_B:].reshape(KV_RANK, H_B * V_B)
    wkv_p = jnp.concatenate([wk, wv], axis=1)
    wa = w_out[:H_A * DH_A][perm]
    wb = w_out[H_A * DH_A:]
    wr = jnp.pad(w_router, ((0, 0), (0, LANES - N_EXPERTS)))
    r_hi = wr.astype(BF16)
    r_lo = (wr - r_hi.astype(F32)).astype(BF16)
    wgu = jnp.concatenate([w_gate, w_up], axis=2).astype(BF16)
    return dict(win=win, wq=wq.astype(BF16), wkv=wkv_p.astype(BF16), wa=wa.astype(BF16),
                wb=wb.astype(BF16), r_hi=r_hi, r_lo=r_lo, wgu=wgu, wd=w_down.astype(BF16))


def _route(aff, bsz, n):
    cap = CAP_FACTOR * n // N_EXPERTS
    a = jnp.swapaxes(aff[:, :N_EXPERTS].reshape(bsz, n, N_EXPERTS), 1, 2)
    vals, idx = lax.top_k(a, cap)
    gidx = idx + (jnp.arange(bsz, dtype=jnp.int32) * n)[:, None, None]
    return (gidx.reshape(bsz * N_EXPERTS, 1, cap).astype(jnp.int32),
            vals.reshape(bsz * N_EXPERTS, 1, cap))


def _tile(n, pref):
    t = pref
    while n % t:
        t //= 2
    return t


def kernel(x, c, ctx, c_ctx, w_mod, b_mod, norm_attn, norm_ffn, w_in, sink, q_norm, kv_norm,
           w_q_up, w_kv_up, w_out, w_router, w_gate, w_up, w_down, norm_final):
    bsz, n, d = x.shape
    lc = ctx.shape[1]
    depth = w_mod.shape[0]
    assert bsz < SUBLANES and n % 256 == 0 and n >= 512 and lc % LANES == 0
    ctx_row = bsz

    cs = jnp.zeros((SUBLANES, d), F32).at[:bsz].set(c).at[ctx_row].set(c_ctx)
    mod = _mod_call(cs, w_mod, b_mod)
    tabs_lat = _rope_tables(n)
    tabs_ctx = _ident_tables(lc)

    tm = _tile(n, 512)
    tq_w = 256
    tq_m = _tile(n, 512)
    tk_m = _tile(n, 512)
    cap = CAP_FACTOR * n // N_EXPERTS
    cap_c = CAP_FACTOR * lc // N_EXPERTS
    ts = _tile(cap, 512)

    xl = x.reshape(bsz * n, d)
    xc = ctx.reshape(bsz * lc, d)
    row2 = lambda a: a.reshape(1, -1)

    for l in range(depth):
        last = l == depth - 1
        w = _prep_layer(w_in[l], w_q_up[l], w_kv_up[l], w_out[l], w_router[l],
                        w_gate[l], w_up[l], w_down[l])
        mod3 = mod[l].reshape(SUBLANES, 1, 6 * d)
        lat_row = lambda i: i // (n // tm)
        ctx_rowf = lambda i: ctx_row
        na, nf = row2(norm_attn[l]), row2(norm_ffn[l])
        qn, kvn = row2(q_norm[l]), row2(kv_norm[l])

        qa, ka, va, qb, kb, vb = _proj_call(xl, mod3, lat_row, na, w["win"], qn, kvn,
                                            w["wq"], w["wkv"], tabs_lat, tm, n // tm)
        qa_c, ka_c, va_c, qb_c, kb_c, vb_c = _proj_call(xc, mod3, ctx_rowf, na, w["win"], qn, kvn,
                                                        w["wq"], w["wkv"], tabs_ctx, lc, 1)
        o_a = _win_call(sink[l], qa, ka_c, va_c, ka, va, bsz, n, lc, tq_w)
        o_b = _mla_call(qb, kb_c, vb_c, kb, vb, bsz, n, lc, tq_m, tk_m)
        xl, h8, aff = _outp_call(o_a, o_b, xl, mod3, lat_row, w["wa"], w["wb"], nf,
                                 w["r_hi"], w["r_lo"], tm)
        idx3, val3 = _route(aff, bsz, n)
        y = _moe_call(idx3, val3, h8, w["wgu"], w["wd"], ts)
        moe = jnp.zeros((bsz * n, d), F32).at[idx3.reshape(-1)].add(y)
        xl = _resid_call(xl, moe, mod3, lat_row, row2(norm_final), tm, last)

        if not last:
            o_a_c = _win_call(sink[l], qa_c, ka_c, va_c, None, None, bsz, lc, lc, lc)
            o_b_c = _mla_call(qb_c, kb_c, vb_c, None, None, bsz, lc, lc, lc, lc)
            xc, h8c, aff_c = _outp_call(o_a_c, o_b_c, xc, mod3, ctx_rowf, w["wa"], w["wb"], nf,
                                        w["r_hi"], w["r_lo"], lc)
            idx3c, val3c = _route(aff_c, bsz, lc)
            yc = _moe_call(idx3c, val3c, h8c, w["wgu"], w["wd"], cap_c)
            moe_c = jnp.zeros((bsz * lc, d), F32).at[idx3c.reshape(-1)].add(yc)
            xc = _resid_call(xc, moe_c, mod3, ctx_rowf, row2(norm_final), lc, False)

    return xl.reshape(bsz, n, d)
```

```python
import functools
import math

import jax
import jax.numpy as jnp
import numpy as np
from jax import lax
from jax.experimental import pallas as pl
from jax.experimental.pallas import tpu as pltpu

F32 = jnp.float32
BF16 = jnp.bfloat16

GRID_W = 64
H_A, KV_A, DH_A = 8, 2, 64
G_A = H_A // KV_A
WINDOW = 128
H_B, NOPE_B, ROPE_B, V_B = 8, 64, 32, 64
Q_RANK, KV_RANK = 384, 256
N_EXPERTS, CAP_FACTOR, D_FF = 16, 2, 512
ROPE_BASE = 10000.0
EPS = 1e-6
LOG2E = math.log2(math.e)
LANES = 128
SUBLANES = 8
NEG_INF = float("-inf")

C_QA, C_KA, C_VA, C_CQ, C_CKV, C_KR, C_END = 0, 512, 640, 768, 1152, 1408, 1536
VMEM_LIMIT = 56 * 1024 * 1024
MOE_VMEM_LIMIT = 60 * 1024 * 1024


def _cparams(sem):
    return pltpu.CompilerParams(dimension_semantics=sem, vmem_limit_bytes=VMEM_LIMIT)


def _nt(a, b):
    return lax.dot_general(a, b, (((1,), (1,)), ((), ())), preferred_element_type=F32)


def _dot(a, b):
    return jnp.dot(a, b, preferred_element_type=F32)


def _split(a):
    hi = a.astype(BF16)
    lo = (a - hi.astype(F32)).astype(BF16)
    return hi, lo


def _dot3(a, b_hi, b_lo):
    a_hi, a_lo = _split(a)
    return _dot(a_hi, b_hi) + (_dot(a_hi, b_lo) + _dot(a_lo, b_hi))


def _lane_tile(t, reps):
    return t if reps == 1 else jnp.concatenate([t] * reps, axis=1)


def _rope(x, cos, sin, half):
    w = x.shape[1]
    reps = w // LANES
    lane = lax.broadcasted_iota(jnp.int32, x.shape, 1)
    first = (lane & (2 * half - 1)) < half
    partner = jnp.where(first, pltpu.roll(x, w - half, 1), pltpu.roll(x, half, 1))
    return x * _lane_tile(cos, reps) + partner * _lane_tile(sin, reps)


def _row_to_col(v):
    n = v.shape[1]
    eye = lax.broadcasted_iota(jnp.int32, (n, n), 0) == lax.broadcasted_iota(jnp.int32, (n, n), 1)
    diag = jnp.where(eye, jnp.broadcast_to(v, (n, n)), 0.0)
    ones = jnp.ones((n, LANES), BF16)
    hi = diag.astype(BF16)
    r1 = diag - hi.astype(F32)
    mid = r1.astype(BF16)
    lo = (r1 - mid.astype(F32)).astype(BF16)
    return _dot(hi, ones) + (_dot(mid, ones) + _dot(lo, ones))


def _rms(x, w):
    ms = jnp.mean(x * x, axis=-1, keepdims=True)
    return x * lax.rsqrt(ms + EPS) * w


def _mod_body(c_ref, w_ref, b_ref, o_ref):
    c = c_ref[...]
    s = c * (1.0 / (1.0 + jnp.exp(-c)))
    w_hi, w_lo = _split(w_ref[0])
    o_ref[0] = _dot3(s, w_hi, w_lo) + b_ref[0]


def _mod_call(cs, w_mod, b_mod):
    depth, d, d6 = w_mod.shape
    tn = 1536
    return pl.pallas_call(
        _mod_body,
        grid=(depth, d6 // tn),
        in_specs=[
            pl.BlockSpec((SUBLANES, d), lambda l, j: (0, 0)),
            pl.BlockSpec((1, d, tn), lambda l, j: (l, 0, j)),
            pl.BlockSpec((1, 1, tn), lambda l, j: (l, 0, j)),
        ],
        out_specs=pl.BlockSpec((1, SUBLANES, tn), lambda l, j: (l, 0, j)),
        out_shape=jax.ShapeDtypeStruct((depth, SUBLANES, d6), F32),
        compiler_params=_cparams(("arbitrary", "arbitrary")),
        name="mod",
    )(cs, w_mod, b_mod.reshape(depth, 1, d6))


def _proj_body(x_ref, mod_ref, nw_ref, win_ref, qn_ref, kvn_ref, wq_ref, wkv_ref,
               caq, saq, cak, sak, cbq, sbq, cbk, sbk,
               qa_o, ka_o, va_o, qb_o, kb_o, vb_o):
    d = x_ref.shape[1]
    m = mod_ref[0]
    y = _rms(x_ref[...], nw_ref[...])
    h = (y * (1.0 + m[:, d:2 * d]) + m[:, 0:d]).astype(BF16)
    p = _dot(h, win_ref[...])
    qa_o[...] = _rope(p[:, C_QA:C_KA], caq[...], saq[...], 16).astype(BF16)
    ka_o[...] = _rope(p[:, C_KA:C_VA], cak[...], sak[...], 16).astype(BF16)
    va_o[...] = p[:, C_VA:C_CQ].astype(BF16)
    cq = _rms(p[:, C_CQ:C_CKV], qn_ref[...]).astype(BF16)
    qb = _dot(cq, wq_ref[...])
    qb_o[...] = _rope(qb, cbq[...], sbq[...], 8).astype(BF16)
    ckv = _rms(p[:, C_CKV:C_KR], kvn_ref[...]).astype(BF16)
    kvb = _dot(ckv, wkv_ref[...])
    kr = _rope(p[:, C_KR:C_END], cbk[...], sbk[...], 8)
    kw = H_B * LANES
    kb_o[...] = (kvb[:, 0:kw] + _lane_tile(kr, H_B)).astype(BF16)
    vlane = lax.broadcasted_iota(jnp.int32, (x_ref.shape[0], kw), 1)
    ones_pad = jnp.where((vlane & (LANES - 1)) >= V_B, 1.0, 0.0)
    vb_o[...] = (kvb[:, kw:] + ones_pad).astype(BF16)


def _proj_call(x2, mod3, mod_row_fn, nw, win, qn, kvn, wq, wkv, tabs, tm, tab_blocks):
    t, d = x2.shape
    full = lambda a: pl.BlockSpec(a.shape, lambda i: (0,) * a.ndim)
    tab_spec = pl.BlockSpec((tm, LANES), lambda i: (i % tab_blocks, 0))
    widths = (512, 128, 128, 1024, 1024, 1024)
    return pl.pallas_call(
        _proj_body,
        grid=(t // tm,),
        in_specs=[
            pl.BlockSpec((tm, d), lambda i: (i, 0)),
            pl.BlockSpec((1, 1, mod3.shape[2]), lambda i: (mod_row_fn(i), 0, 0)),
            full(nw), full(win), full(qn), full(kvn), full(wq), full(wkv),
        ] + [tab_spec] * 8,
        out_specs=[pl.BlockSpec((tm, w), lambda i: (i, 0)) for w in widths],
        out_shape=[jax.ShapeDtypeStruct((t, w), BF16) for w in widths],
        compiler_params=_cparams(("arbitrary",)),
        name="proj",
    )(x2, mod3, nw, win, qn, kvn, wq, wkv, *tabs)


def _win_body(sink_ref, q_ref, kc_ref, vc_ref, *rest, n, tq, band):
    if band:
        k_ref, v_ref, o_ref = rest
    else:
        (o_ref,) = rest
    i = pl.program_id(1)
    kc = kc_ref[...]
    vc = vc_ref[...]
    if band:
        wb = tq + 2 * WINDOW
        start = jnp.clip(i * tq - WINDOW, 0, n - wb)
        start = pl.multiple_of(start, LANES)
        kb = k_ref[pl.ds(start, wb), :]
        vb = v_ref[pl.ds(start, wb), :]
        qpos = i * tq + lax.broadcasted_iota(jnp.int32, (tq, wb), 0)
        kpos = start + lax.broadcasted_iota(jnp.int32, (tq, wb), 1)
        valid = jnp.abs(qpos - kpos) <= WINDOW
    lo = lax.broadcasted_iota(jnp.int32, (tq, LANES), 1) < DH_A
    zero = jnp.zeros((tq, LANES), BF16)
    o_g = []
    for g in range(KV_A):
        sel = lo if g == 0 else jnp.logical_not(lo)
        qg = jnp.concatenate(
            [jnp.where(sel, q_ref[:, p * LANES:(p + 1) * LANES], zero) for p in range(G_A)], axis=0)
        s_c = _nt(qg, kc)
        if band:
            s_b = _nt(qg, kb)
        o_p = []
        for p in range(G_A):
            h = p + g * G_A
            sink2 = sink_ref[h] * LOG2E
            sc = s_c[p * tq:(p + 1) * tq]
            m = jnp.maximum(jnp.max(sc, axis=-1, keepdims=True), sink2)
            if band:
                sb = jnp.where(valid, s_b[p * tq:(p + 1) * tq], NEG_INF)
                m = jnp.maximum(m, jnp.max(sb, axis=-1, keepdims=True))
            pc = jnp.exp2(sc - m)
            den = jnp.sum(pc, axis=-1, keepdims=True) + jnp.exp2(sink2 - m)
            o = _dot(pc.astype(BF16), vc)
            if band:
                pb = jnp.exp2(sb - m)
                den = den + jnp.sum(pb, axis=-1, keepdims=True)
                o = o + _dot(pb.astype(BF16), vb)
            o_p.append(o * (1.0 / den))
        o_g.append(o_p)
    for p in range(G_A):
        o_ref[:, p * LANES:(p + 1) * LANES] = jnp.where(lo, o_g[0][p], o_g[1][p]).astype(BF16)


def _win_call(sink, q, kc, vc, k, v, bsz, n, lc, tq):
    band = k is not None
    nq = n // tq
    in_specs = [
        pl.BlockSpec(memory_space=pltpu.SMEM),
        pl.BlockSpec((tq, H_A * DH_A), lambda b, i: (b * nq + i, 0)),
        pl.BlockSpec((lc, LANES), lambda b, i: (b, 0)),
        pl.BlockSpec((lc, LANES), lambda b, i: (b, 0)),
    ]
    args = [sink, q, kc, vc]
    if band:
        in_specs += [pl.BlockSpec((n, LANES), lambda b, i: (b, 0))] * 2
        args += [k, v]
    return pl.pallas_call(
        functools.partial(_win_body, n=n, tq=tq, band=band),
        grid=(bsz, nq),
        in_specs=in_specs,
        out_specs=pl.BlockSpec((tq, H_A * DH_A), lambda b, i: (b * nq + i, 0)),
        out_shape=jax.ShapeDtypeStruct((bsz * n, H_A * DH_A), BF16),
        compiler_params=_cparams(("arbitrary", "arbitrary")),
        name="win_lat" if band else "win_ctx",
    )(*args)


def _mla_body(q_ref, kc_ref, vc_ref, *rest, n, tk, has_lat):
    if has_lat:
        k_ref, v_ref, o_ref, m_sc, acc_sc = rest
    else:
        (o_ref,) = rest
    tq = q_ref.shape[0]
    lo = lax.broadcasted_iota(jnp.int32, (tq, LANES), 1) < V_B

    def lane_max(p):
        parts = [p[:, c * LANES:(c + 1) * LANES] for c in range(p.shape[1] // LANES)]
        while len(parts) > 1:
            parts = [jnp.maximum(a, b) for a, b in zip(parts[0::2], parts[1::2])] + (
                [parts[-1]] if len(parts) % 2 else [])
        return parts[0]

    def row_max_rep(p):
        r = jnp.max(lane_max(p), axis=-1, keepdims=True)
        return jnp.broadcast_to(r, (tq, LANES))

    def probs(s, m):
        return jnp.exp2((s - _lane_tile(m, s.shape[1] // LANES)).astype(BF16))

    qs, ms, accs = [], [], []
    for hh in range(2):
        hs = slice(hh * LANES, (hh + 1) * LANES)
        q = q_ref[:, hs]
        s = _nt(q, kc_ref[:, hs])
        m = row_max_rep(s)
        qs.append(q)
        ms.append(m)
        accs.append(_dot(probs(s, m), vc_ref[:, hs]))

    if has_lat:
        for hh in range(2):
            m_sc[hh] = ms[hh]
            acc_sc[hh] = accs[hh]

        def chunk(c, carry):
            off = pl.multiple_of(c * tk, tk)
            for hh in range(2):
                hs = slice(hh * LANES, (hh + 1) * LANES)
                s = _nt(qs[hh], k_ref[pl.ds(off, tk), hs])
                m_old = m_sc[hh]
                m_new = jnp.maximum(m_old, row_max_rep(s))
                alpha = jnp.exp2(m_old - m_new)
                m_sc[hh] = m_new
                acc_sc[hh] = alpha * acc_sc[hh] + _dot(probs(s, m_new), v_ref[pl.ds(off, tk), hs])
            return carry

        lax.fori_loop(0, n // tk, chunk, 0)
        accs = [acc_sc[0], acc_sc[1]]

    outs = [a * pltpu.roll(1.0 / a, V_B, 1) for a in accs]
    o_ref[...] = jnp.where(lo, outs[0], pltpu.roll(outs[1], V_B, 1)).astype(BF16)


def _mla_call(q, kc, vc, k, v, bsz, n, lc, tq, tk):
    has_lat = k is not None
    nq = n // tq
    npair = H_B // 2
    in_specs = [
        pl.BlockSpec((tq, 2 * LANES), lambda b, h, i: (b * nq + i, h)),
        pl.BlockSpec((lc, 2 * LANES), lambda b, h, i: (b, h)),
        pl.BlockSpec((lc, 2 * LANES), lambda b, h, i: (b, h)),
    ]
    args = [q, kc, vc]
    scratch = []
    if has_lat:
        in_specs += [pl.BlockSpec((n, 2 * LANES), lambda b, h, i: (b, h))] * 2
        args += [k, v]
        scratch = [pltpu.VMEM((2, tq, LANES), F32)] * 2
    return pl.pallas_call(
        functools.partial(_mla_body, n=n, tk=tk, has_lat=has_lat),
        grid=(bsz, npair, nq),
        in_specs=in_specs,
        out_specs=pl.BlockSpec((tq, LANES), lambda b, h, i: (b * nq + i, h)),
        out_shape=jax.ShapeDtypeStruct((bsz * n, H_B * V_B), BF16),
        scratch_shapes=scratch,
        compiler_params=_cparams(("arbitrary", "arbitrary", "arbitrary")),
        name="mla_lat" if has_lat else "mla_ctx",
    )(*args)


def _outp_body(oa_ref, ob_ref, x_ref, mod_ref, wa_ref, wb_ref, nw_ref, rhi_ref, rlo_ref,
               x_o, h_o, aff_o):
    d = x_ref.shape[1]
    tm = x_ref.shape[0]
    m = mod_ref[0]
    mix = _dot(oa_ref[...], wa_ref[...]) + _dot(ob_ref[...], wb_ref[...])
    x = x_ref[...] + m[:, 2 * d:3 * d] * mix
    x_o[...] = x
    h = _rms(x, nw_ref[...]) * (1.0 + m[:, 4 * d:5 * d]) + m[:, 3 * d:4 * d]
    for s in range(d // LANES):
        h_o[pl.ds(s, tm, stride=SUBLANES), :] = h[:, s * LANES:(s + 1) * LANES]
    logits = _dot3(h, rhi_ref[...], rlo_ref[...])
    lane = lax.broadcasted_iota(jnp.int32, logits.shape, 1)
    logits = jnp.where(lane < N_EXPERTS, logits, NEG_INF)
    e = jnp.exp(logits - jnp.max(logits, axis=-1, keepdims=True))
    aff_o[...] = e / jnp.sum(e, axis=-1, keepdims=True)


def _outp_call(oa, ob, x2, mod3, mod_row_fn, wa, wb, nw, rhi, rlo, tm):
    t, d = x2.shape
    full = lambda a: pl.BlockSpec(a.shape, lambda i: (0,) * a.ndim)
    row = lambda w: pl.BlockSpec((tm, w), lambda i: (i, 0))
    return pl.pallas_call(
        _outp_body,
        grid=(t // tm,),
        in_specs=[row(oa.shape[1]), row(ob.shape[1]), row(d),
                  pl.BlockSpec((1, 1, mod3.shape[2]), lambda i: (mod_row_fn(i), 0, 0)),
                  full(wa), full(wb), full(nw), full(rhi), full(rlo)],
        out_specs=[row(d), pl.BlockSpec((tm * SUBLANES, LANES), lambda i: (i, 0)), row(LANES)],
        out_shape=[jax.ShapeDtypeStruct((t, d), F32),
                   jax.ShapeDtypeStruct((t * SUBLANES, LANES), F32),
                   jax.ShapeDtypeStruct((t, LANES), F32)],
        compiler_params=_cparams(("arbitrary",)),
        name="outp",
    )(oa, ob, x2, mod3, wa, wb, nw, rhi, rlo)


def _moe_body(idx_cur, idx_nxt, val_ref, h_hbm, wgu_ref, wd_ref, out_hbm, buf, ybuf, acc, sem, osem,
              *, ts, nt, n_tok, n_exp):
    be = pl.program_id(0)
    t = pl.program_id(1)
    b = be // n_exp
    e = be % n_exp
    step = be * nt + t
    total = pl.num_programs(0) * nt
    slot = step % 2
    rows = ts * SUBLANES

    def gather(idx_ref, tile, tok0, dst_slot):
        def grp(g, carry):
            for s in range(SUBLANES):
                j = g * SUBLANES + s
                r = pl.multiple_of((idx_ref[0, 0, tile * ts + j] + tok0) * SUBLANES, SUBLANES)
                pltpu.make_async_copy(
                    h_hbm.at[pl.ds(r, SUBLANES), :],
                    buf.at[dst_slot, pl.ds(pl.multiple_of(j * SUBLANES, SUBLANES), SUBLANES), :],
                    sem.at[dst_slot]).start()
            return carry
        lax.fori_loop(0, ts // SUBLANES, grp, 0)

    @pl.when(step == 0)
    def _():
        gather(idx_cur, t, b * n_tok, slot)

    @pl.when(jnp.logical_and(step + 1 < total, t + 1 < nt))
    def _():
        gather(idx_cur, t + 1, b * n_tok, 1 - slot)

    @pl.when(jnp.logical_and(step + 1 < total, t + 1 == nt))
    def _():
        gather(idx_nxt, 0, ((be + 1) // n_exp) * n_tok, 1 - slot)

    @pl.when(jnp.logical_and(e == 0, t == 0))
    def _():
        zrows = min(n_tok * SUBLANES, 4096)

        def zero(i, carry):
            acc[pl.ds(pl.multiple_of(i * zrows, zrows), zrows), :] = jnp.zeros((zrows, LANES), F32)
            return carry
        lax.fori_loop(0, n_tok * SUBLANES // zrows, zero, 0)

    pltpu.make_async_copy(h_hbm.at[pl.ds(0, rows), :], buf.at[slot], sem.at[slot]).wait()

    d = wgu_ref.shape[1]
    x = jnp.concatenate(
        [buf[slot, pl.ds(s, ts, stride=SUBLANES), :] for s in range(d // LANES)], axis=1).astype(BF16)
    au = _dot(x, wgu_ref[0])
    dff = au.shape[1] // 2
    a = au[:, :dff]
    hmid = (a * (1.0 / (1.0 + jnp.exp(-a))) * au[:, dff:]).astype(BF16)
    y = _dot(hmid, wd_ref[0])
    for c in range(d // LANES):
        ybuf[:, c * SUBLANES:(c + 1) * SUBLANES, :] = y[:, c * LANES:(c + 1) * LANES].reshape(
            ts // SUBLANES, SUBLANES, LANES)

    def scatter(g, carry):
        upd = []
        for s in range(SUBLANES):
            j = t * ts + g * SUBLANES + s
            r = pl.multiple_of(idx_cur[0, 0, j] * SUBLANES, SUBLANES)
            yrow = ybuf[g, pl.ds(s, SUBLANES, stride=SUBLANES), :]
            upd.append((r, acc[pl.ds(r, SUBLANES), :] + yrow * val_ref[0, 0, j]))
        for r, v in upd:
            acc[pl.ds(r, SUBLANES), :] = v
        return carry
    lax.fori_loop(0, ts // SUBLANES, scatter, 0)

    @pl.when(jnp.logical_and(e == n_exp - 1, t == nt - 1))
    def _():
        nrows = n_tok * SUBLANES
        flush = pltpu.make_async_copy(
            acc, out_hbm.at[pl.ds(pl.multiple_of(b * nrows, nrows), nrows), :], osem)
        flush.start()
        flush.wait()


def _moe_call(idx3, val3, h8, wgu, wd, ts, n_tok):
    nbe, _, cap = idx3.shape
    nt = cap // ts
    e = wgu.shape[0]
    smem = lambda imap: pl.BlockSpec((1, 1, cap), imap, memory_space=pltpu.SMEM)
    return pl.pallas_call(
        functools.partial(_moe_body, ts=ts, nt=nt, n_tok=n_tok, n_exp=e),
        grid=(nbe, nt),
        in_specs=[
            smem(lambda be, t: (be, 0, 0)),
            smem(lambda be, t: (jnp.minimum(be + 1, nbe - 1), 0, 0)),
            smem(lambda be, t: (be, 0, 0)),
            pl.BlockSpec(memory_space=pl.ANY),
            pl.BlockSpec((1,) + wgu.shape[1:], lambda be, t: (be % e, 0, 0)),
            pl.BlockSpec((1,) + wd.shape[1:], lambda be, t: (be % e, 0, 0)),
        ],
        out_specs=pl.BlockSpec(memory_space=pl.ANY),
        out_shape=jax.ShapeDtypeStruct(h8.shape, F32),
        scratch_shapes=[
            pltpu.VMEM((2, ts * SUBLANES, LANES), F32),
            pltpu.VMEM((ts // SUBLANES, wgu.shape[1] // LANES * SUBLANES, LANES), F32),
            pltpu.VMEM((n_tok * SUBLANES, LANES), F32),
            pltpu.SemaphoreType.DMA((2,)),
            pltpu.SemaphoreType.DMA(()),
        ],
        compiler_params=pltpu.CompilerParams(dimension_semantics=("arbitrary", "arbitrary"),
                                             vmem_limit_bytes=MOE_VMEM_LIMIT),
        name="experts",
    )(idx3, idx3, val3, h8, wgu, wd)


def _resid_body(x_ref, y_ref, mod_ref, nw_ref, o_ref, *, final):
    tm, d = x_ref.shape
    y = jnp.concatenate(
        [y_ref[pl.ds(s, tm, stride=SUBLANES), :] for s in range(d // LANES)], axis=1)
    x = x_ref[...] + mod_ref[0][:, 5 * d:6 * d] * y
    o_ref[...] = _rms(x, nw_ref[...]) if final else x


def _resid_call(x2, y8, mod3, mod_row_fn, nw, tm, final):
    t, d = x2.shape
    row = pl.BlockSpec((tm, d), lambda i: (i, 0))
    return pl.pallas_call(
        functools.partial(_resid_body, final=final),
        grid=(t // tm,),
        in_specs=[row, pl.BlockSpec((tm * SUBLANES, LANES), lambda i: (i, 0)),
                  pl.BlockSpec((1, 1, mod3.shape[2]), lambda i: (mod_row_fn(i), 0, 0)),
                  pl.BlockSpec(nw.shape, lambda i: (0, 0))],
        out_specs=row,
        out_shape=jax.ShapeDtypeStruct((t, d), F32),
        compiler_params=_cparams(("arbitrary",)),
        name="resid",
    )(x2, y8, mod3, nw)


def _rope_tables(n):
    pos_r = (np.arange(n) // GRID_W).astype(np.float32)
    pos_c = (np.arange(n) % GRID_W).astype(np.float32)

    def pattern(nf):
        inv = ROPE_BASE ** (-np.arange(nf, dtype=np.float32) / nf)
        ar = pos_r[:, None] * inv[None, :]
        ac = pos_c[:, None] * inv[None, :]
        cos = np.concatenate([np.cos(ar), np.cos(ar), np.cos(ac), np.cos(ac)], axis=1)
        sin = np.concatenate([-np.sin(ar), np.sin(ar), -np.sin(ac), np.sin(ac)], axis=1)
        return cos.astype(np.float32), sin.astype(np.float32)

    ca, sa = pattern(DH_A // 4)
    cb, sb = pattern(ROPE_B // 4)
    ca2, sa2 = np.tile(ca, (1, 2)), np.tile(sa, (1, 2))
    one64 = np.ones((n, NOPE_B), np.float32)
    z64 = np.zeros((n, NOPE_B), np.float32)
    z32 = np.zeros((n, LANES - NOPE_B - ROPE_B), np.float32)
    cbp = np.concatenate([one64, cb, z32], axis=1)
    sbp = np.concatenate([z64, sb, z32], axis=1)
    sc_a = DH_A ** -0.5 * LOG2E
    sc_b = (NOPE_B + ROPE_B) ** -0.5 * LOG2E
    tabs = [ca2 * sc_a, sa2 * sc_a, ca2, sa2, cbp * sc_b, sbp * sc_b, cbp, sbp]
    return [jnp.asarray(t, F32) for t in tabs]


def _ident_tables(n):
    one = np.ones((n, LANES), np.float32)
    zero = np.zeros((n, LANES), np.float32)
    z32 = np.zeros((n, LANES - NOPE_B - ROPE_B), np.float32)
    onep = np.concatenate([np.ones((n, NOPE_B + ROPE_B), np.float32), z32], axis=1)
    sc_a = DH_A ** -0.5 * LOG2E
    sc_b = (NOPE_B + ROPE_B) ** -0.5 * LOG2E
    tabs = [one * sc_a, zero, one, zero, onep * sc_b, zero, onep, zero]
    return [jnp.asarray(t, F32) for t in tabs]


def _pair_perm():
    cols = []
    for p in range(G_A):
        for h in (p, G_A + p):
            cols.extend(range(h * DH_A, (h + 1) * DH_A))
    return np.asarray(cols)


def _prep_layer(w_in, w_q_up, w_kv_up, w_out, w_router, w_gate, w_up, w_down):
    d = w_in.shape[0]
    perm = _pair_perm()
    o_ka = H_A * DH_A
    o_va = o_ka + KV_A * DH_A
    o_cq = o_va + KV_A * DH_A
    o_ckv = o_cq + Q_RANK
    o_kr = o_ckv + KV_RANK
    z = lambda w: jnp.zeros((d, w), F32)
    win = jnp.concatenate([
        w_in[:, :o_ka][:, perm], w_in[:, o_ka:o_va], w_in[:, o_va:o_cq],
        w_in[:, o_cq:o_ckv], w_in[:, o_ckv:o_kr],
        z(NOPE_B), w_in[:, o_kr:], z(LANES - NOPE_B - ROPE_B)], axis=1).astype(BF16)
    wq = w_q_up.reshape(Q_RANK, H_B, NOPE_B + ROPE_B)
    wq = jnp.pad(wq, ((0, 0), (0, 0), (0, LANES - NOPE_B - ROPE_B))).reshape(Q_RANK, H_B * LANES)
    wkv = w_kv_up.reshape(KV_RANK, H_B, NOPE_B + V_B)
    wk = jnp.pad(wkv[:, :, :NOPE_B], ((0, 0), (0, 0), (0, LANES - NOPE_B))).reshape(KV_RANK, H_B * LANES)
    wv = jnp.pad(wkv[:, :, NOPE_B:], ((0, 0), (0, 0), (0, LANES - V_B))).reshape(KV_RANK, H_B * LANES)
    wkv_p = jnp.concatenate([wk, wv], axis=1)
    wa = w_out[:H_A * DH_A][perm]
    wb = w_out[H_A * DH_A:]
    wr = jnp.pad(w_router, ((0, 0), (0, LANES - N_EXPERTS)))
    r_hi = wr.astype(BF16)
    r_lo = (wr - r_hi.astype(F32)).astype(BF16)
    wgu = jnp.concatenate([w_gate, w_up], axis=2).astype(BF16)
    return dict(win=win, wq=wq.astype(BF16), wkv=wkv_p.astype(BF16), wa=wa.astype(BF16),
                wb=wb.astype(BF16), r_hi=r_hi, r_lo=r_lo, wgu=wgu, wd=w_down.astype(BF16))


def _route(aff, bsz, n):
    cap = CAP_FACTOR * n // N_EXPERTS
    a = jnp.swapaxes(aff[:, :N_EXPERTS].reshape(bsz, n, N_EXPERTS), 1, 2)
    vals, idx = lax.top_k(a, cap)
    return (idx.reshape(bsz * N_EXPERTS, 1, cap).astype(jnp.int32),
            vals.reshape(bsz * N_EXPERTS, 1, cap))


def _tile(n, pref):
    t = pref
    while n % t:
        t //= 2
    return t


def kernel(x, c, ctx, c_ctx, w_mod, b_mod, norm_attn, norm_ffn, w_in, sink, q_norm, kv_norm,
           w_q_up, w_kv_up, w_out, w_router, w_gate, w_up, w_down, norm_final):
    bsz, n, d = x.shape
    lc = ctx.shape[1]
    depth = w_mod.shape[0]
    assert bsz < SUBLANES and n % 256 == 0 and n >= 512 and lc % LANES == 0
    ctx_row = bsz

    cs = jnp.zeros((SUBLANES, d), F32).at[:bsz].set(c).at[ctx_row].set(c_ctx)
    mod = _mod_call(cs, w_mod, b_mod)
    tabs_lat = _rope_tables(n)
    tabs_ctx = _ident_tables(lc)

    tm = _tile(n, 512)
    tq_w = 256
    tq_m = _tile(n, 512)
    tk_m = _tile(n, 2048)
    cap = CAP_FACTOR * n // N_EXPERTS
    cap_c = CAP_FACTOR * lc // N_EXPERTS
    ts = _tile(cap, 512)

    xl = x.reshape(bsz * n, d)
    xc = ctx.reshape(bsz * lc, d)
    row2 = lambda a: a.reshape(1, -1)

    for l in range(depth):
        last = l == depth - 1
        w = _prep_layer(w_in[l], w_q_up[l], w_kv_up[l], w_out[l], w_router[l],
                        w_gate[l], w_up[l], w_down[l])
        mod3 = mod[l].reshape(SUBLANES, 1, 6 * d)
        lat_row = lambda i: i // (n // tm)
        ctx_rowf = lambda i: ctx_row
        na, nf = row2(norm_attn[l]), row2(norm_ffn[l])
        qn, kvn = row2(q_norm[l]), row2(kv_norm[l])

        qa, ka, va, qb, kb, vb = _proj_call(xl, mod3, lat_row, na, w["win"], qn, kvn,
                                            w["wq"], w["wkv"], tabs_lat, tm, n // tm)
        qa_c, ka_c, va_c, qb_c, kb_c, vb_c = _proj_call(xc, mod3, ctx_rowf, na, w["win"], qn, kvn,
                                                        w["wq"], w["wkv"], tabs_ctx, lc, 1)
        o_a = _win_call(sink[l], qa, ka_c, va_c, ka, va, bsz, n, lc, tq_w)
        o_b = _mla_call(qb, kb_c, vb_c, kb, vb, bsz, n, lc, tq_m, tk_m)
        xl, h8, aff = _outp_call(o_a, o_b, xl, mod3, lat_row, w["wa"], w["wb"], nf,
                                 w["r_hi"], w["r_lo"], tm)
        idx3, val3 = _route(aff, bsz, n)
        moe8 = _moe_call(idx3, val3, h8, w["wgu"], w["wd"], ts, n)
        xl = _resid_call(xl, moe8, mod3, lat_row, row2(norm_final), tm, last)

        if not last:
            o_a_c = _win_call(sink[l], qa_c, ka_c, va_c, None, None, bsz, lc, lc, lc)
            o_b_c = _mla_call(qb_c, kb_c, vb_c, None, None, bsz, lc, lc, lc, lc)
            xc, h8c, aff_c = _outp_call(o_a_c, o_b_c, xc, mod3, ctx_rowf, w["wa"], w["wb"], nf,
                                        w["r_hi"], w["r_lo"], lc)
            idx3c, val3c = _route(aff_c, bsz, lc)
            moe8c = _moe_call(idx3c, val3c, h8c, w["wgu"], w["wd"], cap_c, lc)
            xc = _resid_call(xc, moe8c, mod3, ctx_rowf, row2(norm_final), lc, False)

    return xl.reshape(bsz, n, d)
```

```python
import functools
import math

import jax
import jax.numpy as jnp
import numpy as np
from jax import lax
from jax.experimental import pallas as pl
from jax.experimental.pallas import tpu as pltpu

F32 = jnp.float32
BF16 = jnp.bfloat16

GRID_W = 64
H_A, KV_A, DH_A = 8, 2, 64
G_A = H_A // KV_A
WINDOW = 128
H_B, NOPE_B, ROPE_B, V_B = 8, 64, 32, 64
Q_RANK, KV_RANK = 384, 256
N_EXPERTS, CAP_FACTOR, D_FF = 16, 2, 512
ROPE_BASE = 10000.0
EPS = 1e-6
LOG2E = math.log2(math.e)
LANES = 128
SUBLANES = 8
NEG_INF = float("-inf")

C_QA, C_KA, C_VA, C_CQ, C_CKV, C_KR, C_END = 0, 512, 640, 896, 1280, 1536, 1664
VMEM_LIMIT = 56 * 1024 * 1024
MOE_VMEM_LIMIT = 60 * 1024 * 1024


def _cparams(sem):
    return pltpu.CompilerParams(dimension_semantics=sem, vmem_limit_bytes=VMEM_LIMIT)


def _nt(a, b):
    return lax.dot_general(a, b, (((1,), (1,)), ((), ())), preferred_element_type=F32)


def _dot(a, b):
    return jnp.dot(a, b, preferred_element_type=F32)


def _split(a):
    hi = a.astype(BF16)
    lo = (a - hi.astype(F32)).astype(BF16)
    return hi, lo


def _dot3(a, b_hi, b_lo):
    a_hi, a_lo = _split(a)
    return _dot(a_hi, b_hi) + (_dot(a_hi, b_lo) + _dot(a_lo, b_hi))


def _lane_tile(t, reps):
    return t if reps == 1 else jnp.concatenate([t] * reps, axis=1)


def _rope(x, cos, sin, half):
    w = x.shape[1]
    reps = w // LANES
    lane = lax.broadcasted_iota(jnp.int32, x.shape, 1)
    first = (lane & (2 * half - 1)) < half
    partner = jnp.where(first, pltpu.roll(x, w - half, 1), pltpu.roll(x, half, 1))
    return x * _lane_tile(cos, reps) + partner * _lane_tile(sin, reps)


def _rms(x, w):
    ms = jnp.mean(x * x, axis=-1, keepdims=True)
    return x * lax.rsqrt(ms + EPS) * w


def _mod_body(c_ref, w_ref, b_ref, o_ref):
    c = c_ref[...]
    s = c * (1.0 / (1.0 + jnp.exp(-c)))
    w_hi, w_lo = _split(w_ref[0])
    o_ref[0] = _dot3(s, w_hi, w_lo) + b_ref[0]


def _mod_call(cs, w_mod, b_mod):
    depth, d, d6 = w_mod.shape
    tn = 1536
    return pl.pallas_call(
        _mod_body,
        grid=(depth, d6 // tn),
        in_specs=[
            pl.BlockSpec((SUBLANES, d), lambda l, j: (0, 0)),
            pl.BlockSpec((1, d, tn), lambda l, j: (l, 0, j)),
            pl.BlockSpec((1, 1, tn), lambda l, j: (l, 0, j)),
        ],
        out_specs=pl.BlockSpec((1, SUBLANES, tn), lambda l, j: (l, 0, j)),
        out_shape=jax.ShapeDtypeStruct((depth, SUBLANES, d6), F32),
        compiler_params=_cparams(("arbitrary", "arbitrary")),
        name="mod",
    )(cs, w_mod, b_mod.reshape(depth, 1, d6))


def _proj_body(x_ref, mod_ref, nw_ref, win_ref, qn_ref, kvn_ref, wq_ref, wkv_ref,
               caq, saq, cak, sak, cbq, sbq, cbk, sbk,
               qa_o, ka_o, va_o, qb_o, kb_o, vb_o):
    d = x_ref.shape[1]
    m = mod_ref[0]
    y = _rms(x_ref[...], nw_ref[...])
    h = (y * (1.0 + m[:, d:2 * d]) + m[:, 0:d]).astype(BF16)
    p = _dot(h, win_ref[...])
    qa_o[...] = _rope(p[:, C_QA:C_KA], caq[...], saq[...], 16).astype(BF16)
    ka_o[...] = _rope(p[:, C_KA:C_VA], cak[...], sak[...], 16).astype(BF16)
    alane = lax.broadcasted_iota(jnp.int32, (x_ref.shape[0], C_CQ - C_VA), 1)
    va_o[...] = (p[:, C_VA:C_CQ] + jnp.where((alane & (LANES - 1)) >= DH_A, 1.0, 0.0)).astype(BF16)
    cq = _rms(p[:, C_CQ:C_CKV], qn_ref[...]).astype(BF16)
    qb = _dot(cq, wq_ref[...])
    qb_o[...] = _rope(qb, cbq[...], sbq[...], 8).astype(BF16)
    ckv = _rms(p[:, C_CKV:C_KR], kvn_ref[...]).astype(BF16)
    kvb = _dot(ckv, wkv_ref[...])
    kr = _rope(p[:, C_KR:C_END], cbk[...], sbk[...], 8)
    kw = H_B * LANES
    kb_o[...] = (kvb[:, 0:kw] + _lane_tile(kr, H_B)).astype(BF16)
    vlane = lax.broadcasted_iota(jnp.int32, (x_ref.shape[0], kw), 1)
    ones_pad = jnp.where((vlane & (LANES - 1)) >= V_B, 1.0, 0.0)
    vb_o[...] = (kvb[:, kw:] + ones_pad).astype(BF16)


def _proj_call(x2, mod3, mod_row_fn, nw, win, qn, kvn, wq, wkv, tabs, tm, tab_blocks):
    t, d = x2.shape
    full = lambda a: pl.BlockSpec(a.shape, lambda i: (0,) * a.ndim)
    tab_spec = pl.BlockSpec((tm, LANES), lambda i: (i % tab_blocks, 0))
    widths = (512, 128, 256, 1024, 1024, 1024)
    return pl.pallas_call(
        _proj_body,
        grid=(t // tm,),
        in_specs=[
            pl.BlockSpec((tm, d), lambda i: (i, 0)),
            pl.BlockSpec((1, 1, mod3.shape[2]), lambda i: (mod_row_fn(i), 0, 0)),
            full(nw), full(win), full(qn), full(kvn), full(wq), full(wkv),
        ] + [tab_spec] * 8,
        out_specs=[pl.BlockSpec((tm, w), lambda i: (i, 0)) for w in widths],
        out_shape=[jax.ShapeDtypeStruct((t, w), BF16) for w in widths],
        compiler_params=_cparams(("arbitrary",)),
        name="proj",
    )(x2, mod3, nw, win, qn, kvn, wq, wkv, *tabs)


def _win_body(sink_ref, q_ref, kc_ref, vc_ref, *rest, n, tq, band):
    if band:
        k_ref, v_ref, o_ref = rest
    else:
        (o_ref,) = rest
    i = pl.program_id(1)
    kc = kc_ref[...]
    if band:
        wb = tq + 2 * WINDOW
        start = jnp.clip(i * tq - WINDOW, 0, n - wb)
        start = pl.multiple_of(start, LANES)
        kb = k_ref[pl.ds(start, wb), :]
        qpos = i * tq + lax.broadcasted_iota(jnp.int32, (tq, wb), 0)
        kpos = start + lax.broadcasted_iota(jnp.int32, (tq, wb), 1)
        valid = jnp.abs(qpos - kpos) <= WINDOW
    lo = lax.broadcasted_iota(jnp.int32, (tq, LANES), 1) < DH_A
    zero = jnp.zeros((tq, LANES), BF16)
    o_g = []
    for g in range(KV_A):
        sel = lo if g == 0 else jnp.logical_not(lo)
        qg = jnp.concatenate(
            [jnp.where(sel, q_ref[:, p * LANES:(p + 1) * LANES], zero) for p in range(G_A)], axis=0)
        gs = slice(g * LANES, (g + 1) * LANES)
        vc = vc_ref[:, gs]
        s_c = _nt(qg, kc)
        if band:
            s_b = _nt(qg, kb)
            vb = v_ref[pl.ds(start, wb), gs]
        o_p = []
        for p in range(G_A):
            h = p + g * G_A
            sink2 = sink_ref[h] * LOG2E
            sc = s_c[p * tq:(p + 1) * tq]
            m = jnp.maximum(jnp.max(sc, axis=-1, keepdims=True), sink2)
            if band:
                sb = jnp.where(valid, s_b[p * tq:(p + 1) * tq], NEG_INF)
                m = jnp.maximum(m, jnp.max(sb, axis=-1, keepdims=True))
            o = _dot(jnp.exp2((sc - m).astype(BF16)), vc)
            if band:
                o = o + _dot(jnp.exp2((sb - m).astype(BF16)), vb)
            den = pltpu.roll(o, DH_A, 1) + jnp.exp2(sink2 - m)
            o_p.append(o * (1.0 / den))
        o_g.append(o_p)
    for p in range(G_A):
        o_ref[:, p * LANES:(p + 1) * LANES] = jnp.where(
            lo, o_g[0][p], pltpu.roll(o_g[1][p], DH_A, 1)).astype(BF16)


def _win_call(sink, q, kc, vc, k, v, bsz, n, lc, tq):
    band = k is not None
    nq = n // tq
    in_specs = [
        pl.BlockSpec(memory_space=pltpu.SMEM),
        pl.BlockSpec((tq, H_A * DH_A), lambda b, i: (b * nq + i, 0)),
        pl.BlockSpec((lc, LANES), lambda b, i: (b, 0)),
        pl.BlockSpec((lc, KV_A * LANES), lambda b, i: (b, 0)),
    ]
    args = [sink, q, kc, vc]
    if band:
        in_specs += [pl.BlockSpec((n, LANES), lambda b, i: (b, 0)),
                     pl.BlockSpec((n, KV_A * LANES), lambda b, i: (b, 0))]
        args += [k, v]
    return pl.pallas_call(
        functools.partial(_win_body, n=n, tq=tq, band=band),
        grid=(bsz, nq),
        in_specs=in_specs,
        out_specs=pl.BlockSpec((tq, H_A * DH_A), lambda b, i: (b * nq + i, 0)),
        out_shape=jax.ShapeDtypeStruct((bsz * n, H_A * DH_A), BF16),
        compiler_params=_cparams(("arbitrary", "arbitrary")),
        name="win_lat" if band else "win_ctx",
    )(*args)


def _mla_body(q_ref, kc_ref, vc_ref, *rest, n, tk, has_lat):
    if has_lat:
        k_ref, v_ref, o_ref, m_sc, acc_sc = rest
    else:
        (o_ref,) = rest
    tq = q_ref.shape[0]
    lo = lax.broadcasted_iota(jnp.int32, (tq, LANES), 1) < V_B

    def lane_max(p):
        parts = [p[:, c * LANES:(c + 1) * LANES] for c in range(p.shape[1] // LANES)]
        while len(parts) > 1:
            parts = [jnp.maximum(a, b) for a, b in zip(parts[0::2], parts[1::2])] + (
                [parts[-1]] if len(parts) % 2 else [])
        return parts[0]

    def row_max_rep(p):
        r = jnp.max(lane_max(p), axis=-1, keepdims=True)
        return jnp.broadcast_to(r, (tq, LANES))

    def probs(s, m):
        return jnp.exp2((s - _lane_tile(m, s.shape[1] // LANES)).astype(BF16))

    qs, ms, accs = [], [], []
    for hh in range(2):
        hs = slice(hh * LANES, (hh + 1) * LANES)
        q = q_ref[:, hs]
        s = _nt(q, kc_ref[:, hs])
        m = row_max_rep(s)
        qs.append(q)
        ms.append(m)
        accs.append(_dot(probs(s, m), vc_ref[:, hs]))

    if has_lat:
        for hh in range(2):
            m_sc[hh] = ms[hh]
            acc_sc[hh] = accs[hh]

        def chunk(c, carry):
            off = pl.multiple_of(c * tk, tk)
            for hh in range(2):
                hs = slice(hh * LANES, (hh + 1) * LANES)
                s = _nt(qs[hh], k_ref[pl.ds(off, tk), hs])
                m_old = m_sc[hh]
                m_new = jnp.maximum(m_old, row_max_rep(s))
                alpha = jnp.exp2(m_old - m_new)
                m_sc[hh] = m_new
                acc_sc[hh] = alpha * acc_sc[hh] + _dot(probs(s, m_new), v_ref[pl.ds(off, tk), hs])
            return carry

        lax.fori_loop(0, n // tk, chunk, 0)
        accs = [acc_sc[0], acc_sc[1]]

    outs = [a * pltpu.roll(1.0 / a, V_B, 1) for a in accs]
    o_ref[...] = jnp.where(lo, outs[0], pltpu.roll(outs[1], V_B, 1)).astype(BF16)


def _mla_call(q, kc, vc, k, v, bsz, n, lc, tq, tk):
    has_lat = k is not None
    nq = n // tq
    npair = H_B // 2
    in_specs = [
        pl.BlockSpec((tq, 2 * LANES), lambda b, h, i: (b * nq + i, h)),
        pl.BlockSpec((lc, 2 * LANES), lambda b, h, i: (b, h)),
        pl.BlockSpec((lc, 2 * LANES), lambda b, h, i: (b, h)),
    ]
    args = [q, kc, vc]
    scratch = []
    if has_lat:
        in_specs += [pl.BlockSpec((n, 2 * LANES), lambda b, h, i: (b, h))] * 2
        args += [k, v]
        scratch = [pltpu.VMEM((2, tq, LANES), F32)] * 2
    return pl.pallas_call(
        functools.partial(_mla_body, n=n, tk=tk, has_lat=has_lat),
        grid=(bsz, npair, nq),
        in_specs=in_specs,
        out_specs=pl.BlockSpec((tq, LANES), lambda b, h, i: (b * nq + i, h)),
        out_shape=jax.ShapeDtypeStruct((bsz * n, H_B * V_B), BF16),
        scratch_shapes=scratch,
        compiler_params=_cparams(("arbitrary", "arbitrary", "arbitrary")),
        name="mla_lat" if has_lat else "mla_ctx",
    )(*args)


def _outp_body(oa_ref, ob_ref, x_ref, mod_ref, wa_ref, wb_ref, nw_ref, rhi_ref, rlo_ref,
               x_o, h_o, aff_o):
    d = x_ref.shape[1]
    tm = x_ref.shape[0]
    m = mod_ref[0]
    mix = _dot(oa_ref[...], wa_ref[...]) + _dot(ob_ref[...], wb_ref[...])
    x = x_ref[...] + m[:, 2 * d:3 * d] * mix
    x_o[...] = x
    h = _rms(x, nw_ref[...]) * (1.0 + m[:, 4 * d:5 * d]) + m[:, 3 * d:4 * d]
    for s in range(d // LANES):
        h_o[pl.ds(s, tm, stride=SUBLANES), :] = h[:, s * LANES:(s + 1) * LANES]
    logits = _dot3(h, rhi_ref[...], rlo_ref[...])
    lane = lax.broadcasted_iota(jnp.int32, logits.shape, 1)
    logits = jnp.where(lane < N_EXPERTS, logits, NEG_INF)
    e = jnp.exp(logits - jnp.max(logits, axis=-1, keepdims=True))
    aff_o[...] = e / jnp.sum(e, axis=-1, keepdims=True)


def _outp_call(oa, ob, x2, mod3, mod_row_fn, wa, wb, nw, rhi, rlo, tm):
    t, d = x2.shape
    full = lambda a: pl.BlockSpec(a.shape, lambda i: (0,) * a.ndim)
    row = lambda w: pl.BlockSpec((tm, w), lambda i: (i, 0))
    return pl.pallas_call(
        _outp_body,
        grid=(t // tm,),
        in_specs=[row(oa.shape[1]), row(ob.shape[1]), row(d),
                  pl.BlockSpec((1, 1, mod3.shape[2]), lambda i: (mod_row_fn(i), 0, 0)),
                  full(wa), full(wb), full(nw), full(rhi), full(rlo)],
        out_specs=[row(d), pl.BlockSpec((tm * SUBLANES, LANES), lambda i: (i, 0)), row(LANES)],
        out_shape=[jax.ShapeDtypeStruct((t, d), F32),
                   jax.ShapeDtypeStruct((t * SUBLANES, LANES), F32),
                   jax.ShapeDtypeStruct((t, LANES), F32)],
        compiler_params=_cparams(("arbitrary",)),
        name="outp",
    )(oa, ob, x2, mod3, wa, wb, nw, rhi, rlo)


def _route_body(a_ref, lblk_ref, idx_o, val_o, *, cap):
    n_exp, nblk = a_ref.shape[1], a_ref.shape[2]
    capl = idx_o.shape[2]
    rows = n_exp * nblk
    a3 = a_ref[0]
    a2 = a3.reshape(rows, LANES)

    ri = lax.broadcasted_iota(jnp.int32, (LANES, LANES), 0)
    ci = lax.broadcasted_iota(jnp.int32, (LANES, LANES), 1)
    upper = jnp.where(ri <= ci, 1.0, 0.0).astype(BF16)
    eye = jnp.where(ri == ci, 1.0, 0.0).astype(BF16)
    ones = jnp.ones((LANES, LANES), BF16)
    lblk = lblk_ref[...]

    def as01(mask3):
        return jnp.where(mask3, 1.0, 0.0).reshape(rows, LANES).astype(BF16)

    def red(x, op):
        return op(op(x, axis=1, keepdims=True), axis=2, keepdims=True)

    def count(mask3):
        return red(jnp.where(mask3, 1.0, 0.0), jnp.sum)

    def prefix(m01):
        within = _dot(m01, upper)
        tot = _dot(m01, ones)
        off = _dot(lblk, tot.astype(BF16))
        return within, tot, off

    real = a3 >= 0.0
    big = jnp.float32(3.0e38)

    def undecided(state):
        it, lo, hi = state
        return jnp.logical_and(it < 2048, jnp.sum(jnp.where(lo < hi, 1.0, 0.0)) > 0.0)

    def bisect(state):
        it, lo, hi = state
        mid = lo + (hi - lo) * 0.5
        mid = jnp.where(mid >= hi, lo, mid)
        above = a3 > mid
        up = count(above) >= cap
        lo_up = red(jnp.where(above, a3, big), jnp.min)
        hi_dn = red(jnp.where(jnp.logical_or(above, jnp.logical_not(real)), -big, a3), jnp.max)
        return it + 1, jnp.where(up, lo_up, lo), jnp.where(up, hi, hi_dn)

    lo0 = red(jnp.where(real, a3, big), jnp.min)
    hi0 = red(a3, jnp.max)
    _, thr, _ = lax.while_loop(undecided, bisect, (jnp.int32(0), lo0, hi0))

    gt = a3 > thr
    eq = a3 == thr
    need = cap - count(gt)
    eq01 = as01(eq)
    w_eq, _, off_eq = prefix(eq01)
    rank_eq = (off_eq + w_eq - eq01.astype(F32)).reshape(n_exp, nblk, LANES)
    sel01 = as01(jnp.logical_or(gt, jnp.logical_and(eq, rank_eq < need)))
    within, tot, off = prefix(sel01)
    wsel = within * sel01.astype(F32)

    j = lax.broadcasted_iota(jnp.int32, (nblk, capl), 1).astype(F32)
    cbase = (lax.broadcasted_iota(jnp.int32, (nblk, capl), 0) * LANES).astype(F32)
    lidx = lax.broadcasted_iota(jnp.int32, (LANES, capl), 0).astype(F32)
    reps = capl // LANES
    for e in range(n_exp):
        rs = slice(e * nblk, (e + 1) * nblk)
        off_e = _lane_tile(off[rs], reps)
        in_blk = jnp.logical_and(j >= off_e, j < off_e + _lane_tile(tot[rs], reps))
        at = jnp.where(in_blk, 1.0, 0.0).astype(BF16)
        rj = jnp.sum(jnp.where(in_blk, j - off_e + 1.0, 0.0), axis=0, keepdims=True)
        base = jnp.sum(jnp.where(in_blk, cbase, 0.0), axis=0, keepdims=True)
        ranks = _dot(_nt(eye, wsel[rs].astype(BF16)).astype(BF16), at)
        hit = jnp.logical_and(ranks == rj, rj > 0.0)
        local = jnp.sum(jnp.where(hit, lidx, 0.0), axis=0, keepdims=True)
        idx_o[0, e:e + 1, :] = (base + local).astype(jnp.int32)
        a_e = a2[rs]
        hi = a_e.astype(BF16)
        r1 = a_e - hi.astype(F32)
        mid = r1.astype(BF16)
        lo = (r1 - mid.astype(F32)).astype(BF16)
        affs = sum(_dot(_nt(eye, part).astype(BF16), at) for part in (hi, mid, lo))
        val_o[0, e:e + 1, :] = jnp.sum(jnp.where(hit, affs, 0.0), axis=0, keepdims=True)


def _route_call(aff, bsz, n):
    cap = CAP_FACTOR * n // N_EXPERTS
    capl = -(-cap // LANES) * LANES
    nblk = max(n // LANES, 16)
    a = jnp.swapaxes(aff[:, :N_EXPERTS].reshape(bsz, n, N_EXPERTS), 1, 2)
    a = jnp.pad(a, ((0, 0), (0, 0), (0, nblk * LANES - n)), constant_values=-1.0)
    a = a.reshape(bsz, N_EXPERTS, nblk, LANES)
    rows = N_EXPERTS * nblk
    r = np.arange(rows)
    lblk = jnp.asarray((r[:, None] // nblk == r[None, :] // nblk) & (r[None, :] < r[:, None]), BF16)
    idx, val = pl.pallas_call(
        functools.partial(_route_body, cap=cap),
        grid=(bsz,),
        in_specs=[pl.BlockSpec((1, N_EXPERTS, nblk, LANES), lambda b: (b, 0, 0, 0)),
                  pl.BlockSpec((rows, rows), lambda b: (0, 0))],
        out_specs=[pl.BlockSpec((1, N_EXPERTS, capl), lambda b: (b, 0, 0))] * 2,
        out_shape=[jax.ShapeDtypeStruct((bsz, N_EXPERTS, capl), jnp.int32),
                   jax.ShapeDtypeStruct((bsz, N_EXPERTS, capl), F32)],
        compiler_params=_cparams(("arbitrary",)),
        name="route",
    )(a, lblk)
    return (idx[:, :, :cap].reshape(bsz * N_EXPERTS, 1, cap),
            val[:, :, :cap].reshape(bsz * N_EXPERTS, 1, cap))


def _moe_body(idx_cur, nxt_ref, val_ref, h_hbm, wgu_ref, wd_ref, out_hbm,
              buf0, buf1, ybuf, acc, sem0, sem1, osem, *, ts, nt, n_tok, n_exp):
    be = pl.program_id(0)
    t = pl.program_id(1)
    b = be // n_exp
    e = be % n_exp
    step = be * nt + t
    total = pl.num_programs(0) * nt
    rows = ts * SUBLANES

    def whole(buf_ref, sem_ref):
        return pltpu.make_async_copy(h_hbm.at[pl.ds(0, rows), :], buf_ref, sem_ref)

    @pl.when(step == 0)
    def _():
        def grp(g, carry):
            for s in range(SUBLANES):
                j = g * SUBLANES + s
                r = pl.multiple_of((idx_cur[0, 0, j] + b * n_tok) * SUBLANES, SUBLANES)
                pltpu.make_async_copy(
                    h_hbm.at[pl.ds(r, SUBLANES), :],
                    buf0.at[pl.ds(pl.multiple_of(j * SUBLANES, SUBLANES), SUBLANES), :], sem0).start()
            return carry
        lax.fori_loop(0, ts // SUBLANES, grp, 0)

    @pl.when(jnp.logical_and(e == 0, t == 0))
    def _():
        zrows = min(n_tok * SUBLANES, 4096)

        def zero(i, carry):
            acc[pl.ds(pl.multiple_of(i * zrows, zrows), zrows), :] = jnp.zeros((zrows, LANES), F32)
            return carry
        lax.fori_loop(0, n_tok * SUBLANES // zrows, zero, 0)

    def run(src, ssem, dst, dsem):
        whole(src, ssem).wait()
        for j in range(ts):
            pltpu.make_async_copy(
                h_hbm.at[pl.ds(pl.multiple_of(nxt_ref[0, 0, j], SUBLANES), SUBLANES), :],
                dst.at[pl.ds(j * SUBLANES, SUBLANES), :], dsem).start()

        d = wgu_ref.shape[1]
        x = jnp.concatenate(
            [src[pl.ds(s, ts, stride=SUBLANES), :] for s in range(d // LANES)], axis=1).astype(BF16)
        au = _dot(x, wgu_ref[0])
        dff = au.shape[1] // 2
        a = au[:, :dff]
        hmid = (a * (1.0 / (1.0 + jnp.exp(-a))) * au[:, dff:]).astype(BF16)
        y = _dot(hmid, wd_ref[0])
        for c in range(d // LANES):
            ybuf[:, c * SUBLANES:(c + 1) * SUBLANES, :] = y[:, c * LANES:(c + 1) * LANES].reshape(
                ts // SUBLANES, SUBLANES, LANES)

        def scatter(g, carry):
            upd = []
            for s in range(SUBLANES):
                j = t * ts + g * SUBLANES + s
                r = pl.multiple_of(idx_cur[0, 0, j] * SUBLANES, SUBLANES)
                yrow = ybuf[g, pl.ds(s, SUBLANES, stride=SUBLANES), :]
                upd.append((r, acc[pl.ds(r, SUBLANES), :] + yrow * val_ref[0, 0, j]))
            for r, v in upd:
                acc[pl.ds(r, SUBLANES), :] = v
            return carry
        lax.fori_loop(0, ts // SUBLANES, scatter, 0)

        @pl.when(step == total - 1)
        def _():
            whole(dst, dsem).wait()

    @pl.when(step % 2 == 0)
    def _():
        run(buf0, sem0, buf1, sem1)

    @pl.when(step % 2 == 1)
    def _():
        run(buf1, sem1, buf0, sem0)

    @pl.when(jnp.logical_and(e == n_exp - 1, t == nt - 1))
    def _():
        nrows = n_tok * SUBLANES
        flush = pltpu.make_async_copy(
            acc, out_hbm.at[pl.ds(pl.multiple_of(b * nrows, nrows), nrows), :], osem)
        flush.start()
        flush.wait()


def _moe_call(idx3, val3, h8, wgu, wd, ts, n_tok):
    nbe, _, cap = idx3.shape
    nt = cap // ts
    e = wgu.shape[0]
    tok0 = (jnp.arange(nbe, dtype=jnp.int32) // e * n_tok)[:, None, None]
    nxt = jnp.roll(((idx3 + tok0) * SUBLANES).reshape(nbe * nt, 1, ts), -1, axis=0)
    smem = lambda imap: pl.BlockSpec((1, 1, cap), imap, memory_space=pltpu.SMEM)
    return pl.pallas_call(
        functools.partial(_moe_body, ts=ts, nt=nt, n_tok=n_tok, n_exp=e),
        grid=(nbe, nt),
        in_specs=[
            smem(lambda be, t: (be, 0, 0)),
            pl.BlockSpec((1, 1, ts), lambda be, t: (be * nt + t, 0, 0), memory_space=pltpu.SMEM),
            smem(lambda be, t: (be, 0, 0)),
            pl.BlockSpec(memory_space=pl.ANY),
            pl.BlockSpec((1,) + wgu.shape[1:], lambda be, t: (be % e, 0, 0)),
            pl.BlockSpec((1,) + wd.shape[1:], lambda be, t: (be % e, 0, 0)),
        ],
        out_specs=pl.BlockSpec(memory_space=pl.ANY),
        out_shape=jax.ShapeDtypeStruct(h8.shape, F32),
        scratch_shapes=[
            pltpu.VMEM((ts * SUBLANES, LANES), F32),
            pltpu.VMEM((ts * SUBLANES, LANES), F32),
            pltpu.VMEM((ts // SUBLANES, wgu.shape[1] // LANES * SUBLANES, LANES), F32),
            pltpu.VMEM((n_tok * SUBLANES, LANES), F32),
            pltpu.SemaphoreType.DMA(()),
            pltpu.SemaphoreType.DMA(()),
            pltpu.SemaphoreType.DMA(()),
        ],
        compiler_params=pltpu.CompilerParams(dimension_semantics=("arbitrary", "arbitrary"),
                                             vmem_limit_bytes=MOE_VMEM_LIMIT),
        name="experts",
    )(idx3, nxt, val3, h8, wgu, wd)


def _resid_body(x_ref, y_ref, mod_ref, nw_ref, o_ref, *, final):
    tm, d = x_ref.shape
    y = jnp.concatenate(
        [y_ref[pl.ds(s, tm, stride=SUBLANES), :] for s in range(d // LANES)], axis=1)
    x = x_ref[...] + mod_ref[0][:, 5 * d:6 * d] * y
    o_ref[...] = _rms(x, nw_ref[...]) if final else x


def _resid_call(x2, y8, mod3, mod_row_fn, nw, tm, final):
    t, d = x2.shape
    row = pl.BlockSpec((tm, d), lambda i: (i, 0))
    return pl.pallas_call(
        functools.partial(_resid_body, final=final),
        grid=(t // tm,),
        in_specs=[row, pl.BlockSpec((tm * SUBLANES, LANES), lambda i: (i, 0)),
                  pl.BlockSpec((1, 1, mod3.shape[2]), lambda i: (mod_row_fn(i), 0, 0)),
                  pl.BlockSpec(nw.shape, lambda i: (0, 0))],
        out_specs=row,
        out_shape=jax.ShapeDtypeStruct((t, d), F32),
        compiler_params=_cparams(("arbitrary",)),
        name="resid",
    )(x2, y8, mod3, nw)


def _rope_tables(n):
    pos_r = (np.arange(n) // GRID_W).astype(np.float32)
    pos_c = (np.arange(n) % GRID_W).astype(np.float32)

    def pattern(nf):
        inv = ROPE_BASE ** (-np.arange(nf, dtype=np.float32) / nf)
        ar = pos_r[:, None] * inv[None, :]
        ac = pos_c[:, None] * inv[None, :]
        cos = np.concatenate([np.cos(ar), np.cos(ar), np.cos(ac), np.cos(ac)], axis=1)
        sin = np.concatenate([-np.sin(ar), np.sin(ar), -np.sin(ac), np.sin(ac)], axis=1)
        return cos.astype(np.float32), sin.astype(np.float32)

    ca, sa = pattern(DH_A // 4)
    cb, sb = pattern(ROPE_B // 4)
    ca2, sa2 = np.tile(ca, (1, 2)), np.tile(sa, (1, 2))
    one64 = np.ones((n, NOPE_B), np.float32)
    z64 = np.zeros((n, NOPE_B), np.float32)
    z32 = np.zeros((n, LANES - NOPE_B - ROPE_B), np.float32)
    cbp = np.concatenate([one64, cb, z32], axis=1)
    sbp = np.concatenate([z64, sb, z32], axis=1)
    sc_a = DH_A ** -0.5 * LOG2E
    sc_b = (NOPE_B + ROPE_B) ** -0.5 * LOG2E
    tabs = [ca2 * sc_a, sa2 * sc_a, ca2, sa2, cbp * sc_b, sbp * sc_b, cbp, sbp]
    return [jnp.asarray(t, F32) for t in tabs]


def _ident_tables(n):
    one = np.ones((n, LANES), np.float32)
    zero = np.zeros((n, LANES), np.float32)
    z32 = np.zeros((n, LANES - NOPE_B - ROPE_B), np.float32)
    onep = np.concatenate([np.ones((n, NOPE_B + ROPE_B), np.float32), z32], axis=1)
    sc_a = DH_A ** -0.5 * LOG2E
    sc_b = (NOPE_B + ROPE_B) ** -0.5 * LOG2E
    tabs = [one * sc_a, zero, one, zero, onep * sc_b, zero, onep, zero]
    return [jnp.asarray(t, F32) for t in tabs]


def _pair_perm():
    cols = []
    for p in range(G_A):
        for h in (p, G_A + p):
            cols.extend(range(h * DH_A, (h + 1) * DH_A))
    return np.asarray(cols)


def _prep_layer(w_in, w_q_up, w_kv_up, w_out, w_router, w_gate, w_up, w_down):
    d = w_in.shape[0]
    perm = _pair_perm()
    o_ka = H_A * DH_A
    o_va = o_ka + KV_A * DH_A
    o_cq = o_va + KV_A * DH_A
    o_ckv = o_cq + Q_RANK
    o_kr = o_ckv + KV_RANK
    z = lambda w: jnp.zeros((d, w), F32)
    win = jnp.concatenate([
        w_in[:, :o_ka][:, perm], w_in[:, o_ka:o_va],
        w_in[:, o_va:o_va + DH_A], z(LANES - DH_A), w_in[:, o_va + DH_A:o_cq], z(LANES - DH_A),
        w_in[:, o_cq:o_ckv], w_in[:, o_ckv:o_kr],
        z(NOPE_B), w_in[:, o_kr:], z(LANES - NOPE_B - ROPE_B)], axis=1).astype(BF16)
    wq = w_q_up.reshape(Q_RANK, H_B, NOPE_B + ROPE_B)
    wq = jnp.pad(wq, ((0, 0), (0, 0), (0, LANES - NOPE_B - ROPE_B))).reshape(Q_RANK, H_B * LANES)
    wkv = w_kv_up.reshape(KV_RANK, H_B, NOPE_B + V_B)
    wk = jnp.pad(wkv[:, :, :NOPE_B], ((0, 0), (0, 0), (0, LANES - NOPE_B))).reshape(KV_RANK, H_B * LANES)
    wv = jnp.pad(wkv[:, :, NOPE_B:], ((0, 0), (0, 0), (0, LANES - V_B))).reshape(KV_RANK, H_B * LANES)
    wkv_p = jnp.concatenate([wk, wv], axis=1)
    wa = w_out[:H_A * DH_A][perm]
    wb = w_out[H_A * DH_A:]
    wr = jnp.pad(w_router, ((0, 0), (0, LANES - N_EXPERTS)))
    r_hi = wr.astype(BF16)
    r_lo = (wr - r_hi.astype(F32)).astype(BF16)
    wgu = jnp.concatenate([w_gate, w_up], axis=2).astype(BF16)
    return dict(win=win, wq=wq.astype(BF16), wkv=wkv_p.astype(BF16), wa=wa.astype(BF16),
                wb=wb.astype(BF16), r_hi=r_hi, r_lo=r_lo, wgu=wgu, wd=w_down.astype(BF16))


def _tile(n, pref):
    t = pref
    while n % t:
        t //= 2
    return t


def kernel(x, c, ctx, c_ctx, w_mod, b_mod, norm_attn, norm_ffn, w_in, sink, q_norm, kv_norm,
           w_q_up, w_kv_up, w_out, w_router, w_gate, w_up, w_down, norm_final):
    bsz, n, d = x.shape
    lc = ctx.shape[1]
    depth = w_mod.shape[0]
    assert bsz < SUBLANES and n % 256 == 0 and n >= 512 and lc % LANES == 0
    ctx_row = bsz

    cs = jnp.zeros((SUBLANES, d), F32).at[:bsz].set(c).at[ctx_row].set(c_ctx)
    mod = _mod_call(cs, w_mod, b_mod)
    tabs_lat = _rope_tables(n)
    tabs_ctx = _ident_tables(lc)

    tm = _tile(n, 512)
    tq_w = 256
    tq_m = _tile(n, 512)
    tk_m = _tile(n, 2048)
    cap = CAP_FACTOR * n // N_EXPERTS
    cap_c = CAP_FACTOR * lc // N_EXPERTS
    ts = _tile(cap, 512)

    xl = x.reshape(bsz * n, d)
    xc = ctx.reshape(bsz * lc, d)
    row2 = lambda a: a.reshape(1, -1)

    for l in range(depth):
        last = l == depth - 1
        w = _prep_layer(w_in[l], w_q_up[l], w_kv_up[l], w_out[l], w_router[l],
                        w_gate[l], w_up[l], w_down[l])
        mod3 = mod[l].reshape(SUBLANES, 1, 6 * d)
        lat_row = lambda i: i // (n // tm)
        ctx_rowf = lambda i: ctx_row
        na, nf = row2(norm_attn[l]), row2(norm_ffn[l])
        qn, kvn = row2(q_norm[l]), row2(kv_norm[l])

        qa, ka, va, qb, kb, vb = _proj_call(xl, mod3, lat_row, na, w["win"], qn, kvn,
                                            w["wq"], w["wkv"], tabs_lat, tm, n // tm)
        qa_c, ka_c, va_c, qb_c, kb_c, vb_c = _proj_call(xc, mod3, ctx_rowf, na, w["win"], qn, kvn,
                                                        w["wq"], w["wkv"], tabs_ctx, lc, 1)
        o_a = _win_call(sink[l], qa, ka_c, va_c, ka, va, bsz, n, lc, tq_w)
        o_b = _mla_call(qb, kb_c, vb_c, kb, vb, bsz, n, lc, tq_m, tk_m)
        xl, h8, aff = _outp_call(o_a, o_b, xl, mod3, lat_row, w["wa"], w["wb"], nf,
                                 w["r_hi"], w["r_lo"], tm)
        idx3, val3 = _route_call(aff, bsz, n)
        moe8 = _moe_call(idx3, val3, h8, w["wgu"], w["wd"], ts, n)
        xl = _resid_call(xl, moe8, mod3, lat_row, row2(norm_final), tm, last)

        if not last:
            o_a_c = _win_call(sink[l], qa_c, ka_c, va_c, None, None, bsz, lc, lc, lc)
            o_b_c = _mla_call(qb_c, kb_c, vb_c, None, None, bsz, lc, lc, lc, lc)
            xc, h8c, aff_c = _outp_call(o_a_c, o_b_c, xc, mod3, ctx_rowf, w["wa"], w["wb"], nf,
                                        w["r_hi"], w["r_lo"], lc)
            idx3c, val3c = _route_call(aff_c, bsz, lc)
            moe8c = _moe_call(idx3c, val3c, h8c, w["wgu"], w["wd"], cap_c, lc)
            xc = _resid_call(xc, moe8c, mod3, ctx_rowf, row2(norm_final), lc, False)

    return xl.reshape(bsz, n, d)
```

```python
import functools
import math

import jax
import jax.numpy as jnp
import numpy as np
from jax import lax
from jax.experimental import pallas as pl
from jax.experimental.pallas import tpu as pltpu

F32 = jnp.float32
BF16 = jnp.bfloat16

GRID_W = 64
H_A, KV_A, DH_A = 8, 2, 64
G_A = H_A // KV_A
WINDOW = 128
H_B, NOPE_B, ROPE_B, V_B = 8, 64, 32, 64
Q_RANK, KV_RANK = 384, 256
N_EXPERTS, CAP_FACTOR, D_FF = 16, 2, 512
ROPE_BASE = 10000.0
EPS = 1e-6
LOG2E = math.log2(math.e)
LANES = 128
SUBLANES = 8
NEG_INF = float("-inf")

C_QA, C_KA, C_VA, C_CQ, C_CKV, C_KR, C_END = 0, 512, 640, 896, 1280, 1536, 1664
VMEM_LIMIT = 56 * 1024 * 1024
MOE_VMEM_LIMIT = 60 * 1024 * 1024


def _cparams(sem):
    return pltpu.CompilerParams(dimension_semantics=sem, vmem_limit_bytes=VMEM_LIMIT)


def _nt(a, b):
    return lax.dot_general(a, b, (((1,), (1,)), ((), ())), preferred_element_type=F32)


def _dot(a, b):
    return jnp.dot(a, b, preferred_element_type=F32)


def _split(a):
    hi = a.astype(BF16)
    lo = (a - hi.astype(F32)).astype(BF16)
    return hi, lo


def _dot3(a, b_hi, b_lo):
    a_hi, a_lo = _split(a)
    return _dot(a_hi, b_hi) + (_dot(a_hi, b_lo) + _dot(a_lo, b_hi))


def _lane_tile(t, reps):
    return t if reps == 1 else jnp.concatenate([t] * reps, axis=1)


def _rope(x, cos, sin, half):
    w = x.shape[1]
    reps = w // LANES
    lane = lax.broadcasted_iota(jnp.int32, x.shape, 1)
    first = (lane & (2 * half - 1)) < half
    partner = jnp.where(first, pltpu.roll(x, w - half, 1), pltpu.roll(x, half, 1))
    return x * _lane_tile(cos, reps) + partner * _lane_tile(sin, reps)


def _rms(x, w):
    ms = jnp.mean(x * x, axis=-1, keepdims=True)
    return x * lax.rsqrt(ms + EPS) * w


def _mod_body(c_ref, w_ref, b_ref, o_ref):
    c = c_ref[...]
    s = c * (1.0 / (1.0 + jnp.exp(-c)))
    w_hi, w_lo = _split(w_ref[0])
    o_ref[0] = _dot3(s, w_hi, w_lo) + b_ref[0]


def _mod_call(cs, w_mod, b_mod):
    depth, d, d6 = w_mod.shape
    tn = 1536
    return pl.pallas_call(
        _mod_body,
        grid=(depth, d6 // tn),
        in_specs=[
            pl.BlockSpec((SUBLANES, d), lambda l, j: (0, 0)),
            pl.BlockSpec((1, d, tn), lambda l, j: (l, 0, j)),
            pl.BlockSpec((1, 1, tn), lambda l, j: (l, 0, j)),
        ],
        out_specs=pl.BlockSpec((1, SUBLANES, tn), lambda l, j: (l, 0, j)),
        out_shape=jax.ShapeDtypeStruct((depth, SUBLANES, d6), F32),
        compiler_params=_cparams(("arbitrary", "arbitrary")),
        name="mod",
    )(cs, w_mod, b_mod.reshape(depth, 1, d6))


def _proj_body(x_ref, mod_ref, nw_ref, win_ref, qn_ref, kvn_ref, wq_ref, wkv_ref,
               caq, saq, cak, sak, cbq, sbq, cbk, sbk,
               qa_o, ka_o, va_o, qb_o, kb_o, vb_o):
    d = x_ref.shape[1]
    m = mod_ref[0]
    y = _rms(x_ref[...], nw_ref[...])
    h = (y * (1.0 + m[:, d:2 * d]) + m[:, 0:d]).astype(BF16)
    p = _dot(h, win_ref[...])
    qa_o[...] = _rope(p[:, C_QA:C_KA], caq[...], saq[...], 16).astype(BF16)
    ka_o[...] = _rope(p[:, C_KA:C_VA], cak[...], sak[...], 16).astype(BF16)
    alane = lax.broadcasted_iota(jnp.int32, (x_ref.shape[0], C_CQ - C_VA), 1)
    va_o[...] = (p[:, C_VA:C_CQ] + jnp.where((alane & (LANES - 1)) >= DH_A, 1.0, 0.0)).astype(BF16)
    cq = _rms(p[:, C_CQ:C_CKV], qn_ref[...]).astype(BF16)
    qb = _dot(cq, wq_ref[...])
    qb_o[...] = _rope(qb, cbq[...], sbq[...], 8).astype(BF16)
    ckv = _rms(p[:, C_CKV:C_KR], kvn_ref[...]).astype(BF16)
    kvb = _dot(ckv, wkv_ref[...])
    kr = _rope(p[:, C_KR:C_END], cbk[...], sbk[...], 8)
    kw = H_B * LANES
    kb_o[...] = (kvb[:, 0:kw] + _lane_tile(kr, H_B)).astype(BF16)
    vlane = lax.broadcasted_iota(jnp.int32, (x_ref.shape[0], kw), 1)
    ones_pad = jnp.where((vlane & (LANES - 1)) >= V_B, 1.0, 0.0)
    vb_o[...] = (kvb[:, kw:] + ones_pad).astype(BF16)


def _proj_call(x2, mod3, mod_row_fn, nw, win, qn, kvn, wq, wkv, tabs, tm, tab_blocks):
    t, d = x2.shape
    full = lambda a: pl.BlockSpec(a.shape, lambda i: (0,) * a.ndim)
    tab_spec = pl.BlockSpec((tm, LANES), lambda i: (i % tab_blocks, 0))
    widths = (512, 128, 256, 1024, 1024, 1024)
    return pl.pallas_call(
        _proj_body,
        grid=(t // tm,),
        in_specs=[
            pl.BlockSpec((tm, d), lambda i: (i, 0)),
            pl.BlockSpec((1, 1, mod3.shape[2]), lambda i: (mod_row_fn(i), 0, 0)),
            full(nw), full(win), full(qn), full(kvn), full(wq), full(wkv),
        ] + [tab_spec] * 8,
        out_specs=[pl.BlockSpec((tm, w), lambda i: (i, 0)) for w in widths],
        out_shape=[jax.ShapeDtypeStruct((t, w), BF16) for w in widths],
        compiler_params=_cparams(("arbitrary",)),
        name="proj",
    )(x2, mod3, nw, win, qn, kvn, wq, wkv, *tabs)


def _win_body(sink_ref, q_ref, kc_ref, vc_ref, *rest, n, tq, band):
    if band:
        k_ref, v_ref, o_ref = rest
    else:
        (o_ref,) = rest
    i = pl.program_id(1)
    kc = kc_ref[...]
    if band:
        wb = tq + 2 * WINDOW
        start = jnp.clip(i * tq - WINDOW, 0, n - wb)
        start = pl.multiple_of(start, LANES)
        kb = k_ref[pl.ds(start, wb), :]
        qpos = i * tq + lax.broadcasted_iota(jnp.int32, (tq, wb), 0)
        kpos = start + lax.broadcasted_iota(jnp.int32, (tq, wb), 1)
        valid = jnp.abs(qpos - kpos) <= WINDOW
    lo = lax.broadcasted_iota(jnp.int32, (tq, LANES), 1) < DH_A
    zero = jnp.zeros((tq, LANES), BF16)
    o_g = []
    for g in range(KV_A):
        sel = lo if g == 0 else jnp.logical_not(lo)
        qg = jnp.concatenate(
            [jnp.where(sel, q_ref[:, p * LANES:(p + 1) * LANES], zero) for p in range(G_A)], axis=0)
        gs = slice(g * LANES, (g + 1) * LANES)
        vc = vc_ref[:, gs]
        s_c = _nt(qg, kc)
        if band:
            s_b = _nt(qg, kb)
            vb = v_ref[pl.ds(start, wb), gs]
        o_p = []
        for p in range(G_A):
            h = p + g * G_A
            sink2 = sink_ref[h] * LOG2E
            sc = s_c[p * tq:(p + 1) * tq]
            m = jnp.maximum(jnp.max(sc, axis=-1, keepdims=True), sink2)
            if band:
                sb = jnp.where(valid, s_b[p * tq:(p + 1) * tq], NEG_INF)
                m = jnp.maximum(m, jnp.max(sb, axis=-1, keepdims=True))
            o = _dot(jnp.exp2((sc - m).astype(BF16)), vc)
            if band:
                o = o + _dot(jnp.exp2((sb - m).astype(BF16)), vb)
            den = pltpu.roll(o, DH_A, 1) + jnp.exp2(sink2 - m)
            o_p.append(o * (1.0 / den))
        o_g.append(o_p)
    for p in range(G_A):
        o_ref[:, p * LANES:(p + 1) * LANES] = jnp.where(
            lo, o_g[0][p], pltpu.roll(o_g[1][p], DH_A, 1)).astype(BF16)


def _win_call(sink, q, kc, vc, k, v, bsz, n, lc, tq):
    band = k is not None
    nq = n // tq
    in_specs = [
        pl.BlockSpec(memory_space=pltpu.SMEM),
        pl.BlockSpec((tq, H_A * DH_A), lambda b, i: (b * nq + i, 0)),
        pl.BlockSpec((lc, LANES), lambda b, i: (b, 0)),
        pl.BlockSpec((lc, KV_A * LANES), lambda b, i: (b, 0)),
    ]
    args = [sink, q, kc, vc]
    if band:
        in_specs += [pl.BlockSpec((n, LANES), lambda b, i: (b, 0)),
                     pl.BlockSpec((n, KV_A * LANES), lambda b, i: (b, 0))]
        args += [k, v]
    return pl.pallas_call(
        functools.partial(_win_body, n=n, tq=tq, band=band),
        grid=(bsz, nq),
        in_specs=in_specs,
        out_specs=pl.BlockSpec((tq, H_A * DH_A), lambda b, i: (b * nq + i, 0)),
        out_shape=jax.ShapeDtypeStruct((bsz * n, H_A * DH_A), BF16),
        compiler_params=_cparams(("arbitrary", "arbitrary")),
        name="win_lat" if band else "win_ctx",
    )(*args)


def _mla_body(q_ref, kc_ref, vc_ref, *rest, n, tk, has_lat):
    if has_lat:
        k_ref, v_ref, o_ref, m_sc, acc_sc = rest
    else:
        (o_ref,) = rest
    tq = q_ref.shape[0]
    lo = lax.broadcasted_iota(jnp.int32, (tq, LANES), 1) < V_B

    def lane_max(p):
        parts = [p[:, c * LANES:(c + 1) * LANES] for c in range(p.shape[1] // LANES)]
        while len(parts) > 1:
            parts = [jnp.maximum(a, b) for a, b in zip(parts[0::2], parts[1::2])] + (
                [parts[-1]] if len(parts) % 2 else [])
        return parts[0]

    def row_max_rep(p):
        r = jnp.max(lane_max(p), axis=-1, keepdims=True)
        return jnp.broadcast_to(r, (tq, LANES))

    def probs(s, m):
        return jnp.exp2((s - _lane_tile(m, s.shape[1] // LANES)).astype(BF16))

    qs, ms, accs = [], [], []
    for hh in range(2):
        hs = slice(hh * LANES, (hh + 1) * LANES)
        q = q_ref[:, hs]
        s = _nt(q, kc_ref[:, hs])
        m = row_max_rep(s)
        qs.append(q)
        ms.append(m)
        accs.append(_dot(probs(s, m), vc_ref[:, hs]))

    if has_lat:
        for hh in range(2):
            m_sc[hh] = ms[hh]
            acc_sc[hh] = accs[hh]

        def chunk(c, carry):
            off = pl.multiple_of(c * tk, tk)
            for hh in range(2):
                hs = slice(hh * LANES, (hh + 1) * LANES)
                s = _nt(qs[hh], k_ref[pl.ds(off, tk), hs])
                m_old = m_sc[hh]
                m_new = jnp.maximum(m_old, row_max_rep(s))
                alpha = jnp.exp2(m_old - m_new)
                m_sc[hh] = m_new
                acc_sc[hh] = alpha * acc_sc[hh] + _dot(probs(s, m_new), v_ref[pl.ds(off, tk), hs])
            return carry

        lax.fori_loop(0, n // tk, chunk, 0)
        accs = [acc_sc[0], acc_sc[1]]

    outs = [a * pltpu.roll(1.0 / a, V_B, 1) for a in accs]
    o_ref[...] = jnp.where(lo, outs[0], pltpu.roll(outs[1], V_B, 1)).astype(BF16)


def _mla_call(q, kc, vc, k, v, bsz, n, lc, tq, tk):
    has_lat = k is not None
    nq = n // tq
    npair = H_B // 2
    in_specs = [
        pl.BlockSpec((tq, 2 * LANES), lambda b, h, i: (b * nq + i, h)),
        pl.BlockSpec((lc, 2 * LANES), lambda b, h, i: (b, h)),
        pl.BlockSpec((lc, 2 * LANES), lambda b, h, i: (b, h)),
    ]
    args = [q, kc, vc]
    scratch = []
    if has_lat:
        in_specs += [pl.BlockSpec((n, 2 * LANES), lambda b, h, i: (b, h))] * 2
        args += [k, v]
        scratch = [pltpu.VMEM((2, tq, LANES), F32)] * 2
    return pl.pallas_call(
        functools.partial(_mla_body, n=n, tk=tk, has_lat=has_lat),
        grid=(bsz, npair, nq),
        in_specs=in_specs,
        out_specs=pl.BlockSpec((tq, LANES), lambda b, h, i: (b * nq + i, h)),
        out_shape=jax.ShapeDtypeStruct((bsz * n, H_B * V_B), BF16),
        scratch_shapes=scratch,
        compiler_params=_cparams(("arbitrary", "arbitrary", "arbitrary")),
        name="mla_lat" if has_lat else "mla_ctx",
    )(*args)


def _outp_body(oa_ref, ob_ref, x_ref, mod_ref, wa_ref, wb_ref, nw_ref, rhi_ref, rlo_ref,
               x_o, h_o, aff_o):
    d = x_ref.shape[1]
    tm = x_ref.shape[0]
    m = mod_ref[0]
    mix = _dot(oa_ref[...], wa_ref[...]) + _dot(ob_ref[...], wb_ref[...])
    x = x_ref[...] + m[:, 2 * d:3 * d] * mix
    x_o[...] = x
    h = _rms(x, nw_ref[...]) * (1.0 + m[:, 4 * d:5 * d]) + m[:, 3 * d:4 * d]
    for s in range(d // LANES):
        h_o[pl.ds(s, tm, stride=SUBLANES), :] = h[:, s * LANES:(s + 1) * LANES]
    logits = _dot3(h, rhi_ref[...], rlo_ref[...])
    lane = lax.broadcasted_iota(jnp.int32, logits.shape, 1)
    logits = jnp.where(lane < N_EXPERTS, logits, NEG_INF)
    e = jnp.exp(logits - jnp.max(logits, axis=-1, keepdims=True))
    aff_o[...] = e / jnp.sum(e, axis=-1, keepdims=True)


def _outp_call(oa, ob, x2, mod3, mod_row_fn, wa, wb, nw, rhi, rlo, tm):
    t, d = x2.shape
    full = lambda a: pl.BlockSpec(a.shape, lambda i: (0,) * a.ndim)
    row = lambda w: pl.BlockSpec((tm, w), lambda i: (i, 0))
    return pl.pallas_call(
        _outp_body,
        grid=(t // tm,),
        in_specs=[row(oa.shape[1]), row(ob.shape[1]), row(d),
                  pl.BlockSpec((1, 1, mod3.shape[2]), lambda i: (mod_row_fn(i), 0, 0)),
                  full(wa), full(wb), full(nw), full(rhi), full(rlo)],
        out_specs=[row(d), pl.BlockSpec((tm * SUBLANES, LANES), lambda i: (i, 0)), row(LANES)],
        out_shape=[jax.ShapeDtypeStruct((t, d), F32),
                   jax.ShapeDtypeStruct((t * SUBLANES, LANES), F32),
                   jax.ShapeDtypeStruct((t, LANES), F32)],
        compiler_params=_cparams(("arbitrary",)),
        name="outp",
    )(oa, ob, x2, mod3, wa, wb, nw, rhi, rlo)


def _route_body(a_ref, lblk_ref, idx_o, val_o, *, cap):
    n_exp, nblk = a_ref.shape[1], a_ref.shape[2]
    capl = idx_o.shape[2]
    rows = n_exp * nblk
    a3 = a_ref[0]
    a2 = a3.reshape(rows, LANES)

    ri = lax.broadcasted_iota(jnp.int32, (LANES, LANES), 0)
    ci = lax.broadcasted_iota(jnp.int32, (LANES, LANES), 1)
    upper = jnp.where(ri <= ci, 1.0, 0.0).astype(BF16)
    eye = jnp.where(ri == ci, 1.0, 0.0).astype(BF16)
    ones = jnp.ones((LANES, LANES), BF16)
    lblk = lblk_ref[...]

    def as01(mask3):
        return jnp.where(mask3, 1.0, 0.0).reshape(rows, LANES).astype(BF16)

    def red(x, op):
        return op(op(x, axis=1, keepdims=True), axis=2, keepdims=True)

    def count(mask3):
        return red(jnp.where(mask3, 1.0, 0.0), jnp.sum)

    def prefix(m01):
        within = _dot(m01, upper)
        tot = _dot(m01, ones)
        off = _dot(lblk, tot.astype(BF16))
        return within, tot, off

    real = a3 >= 0.0
    big = jnp.float32(3.0e38)

    def undecided(state):
        it, lo, hi = state
        return jnp.logical_and(it < 2048, jnp.sum(jnp.where(lo < hi, 1.0, 0.0)) > 0.0)

    def bisect(state):
        it, lo, hi = state
        mid = lo + (hi - lo) * 0.5
        mid = jnp.where(mid >= hi, lo, mid)
        above = a3 > mid
        up = count(above) >= cap
        lo_up = red(jnp.where(above, a3, big), jnp.min)
        hi_dn = red(jnp.where(jnp.logical_or(above, jnp.logical_not(real)), -big, a3), jnp.max)
        return it + 1, jnp.where(up, lo_up, lo), jnp.where(up, hi, hi_dn)

    lo0 = red(jnp.where(real, a3, big), jnp.min)
    hi0 = red(a3, jnp.max)
    _, thr, _ = lax.while_loop(undecided, bisect, (jnp.int32(0), lo0, hi0))

    gt = a3 > thr
    eq = a3 == thr
    need = cap - count(gt)
    eq01 = as01(eq)
    w_eq, _, off_eq = prefix(eq01)
    rank_eq = (off_eq + w_eq - eq01.astype(F32)).reshape(n_exp, nblk, LANES)
    sel01 = as01(jnp.logical_or(gt, jnp.logical_and(eq, rank_eq < need)))
    within, tot, off = prefix(sel01)
    wsel = within * sel01.astype(F32)

    j = lax.broadcasted_iota(jnp.int32, (nblk, capl), 1).astype(F32)
    cbase = (lax.broadcasted_iota(jnp.int32, (nblk, capl), 0) * LANES).astype(F32)
    lidx = lax.broadcasted_iota(jnp.int32, (LANES, capl), 0).astype(F32)
    reps = capl // LANES
    for e in range(n_exp):
        rs = slice(e * nblk, (e + 1) * nblk)
        off_e = _lane_tile(off[rs], reps)
        in_blk = jnp.logical_and(j >= off_e, j < off_e + _lane_tile(tot[rs], reps))
        at = jnp.where(in_blk, 1.0, 0.0).astype(BF16)
        rj = jnp.sum(jnp.where(in_blk, j - off_e + 1.0, 0.0), axis=0, keepdims=True)
        base = jnp.sum(jnp.where(in_blk, cbase, 0.0), axis=0, keepdims=True)
        ranks = _dot(_nt(eye, wsel[rs].astype(BF16)).astype(BF16), at)
        hit = jnp.logical_and(ranks == rj, rj > 0.0)
        local = jnp.sum(jnp.where(hit, lidx, 0.0), axis=0, keepdims=True)
        idx_o[0, e:e + 1, :] = (base + local).astype(jnp.int32)
        a_e = a2[rs]
        hi = a_e.astype(BF16)
        r1 = a_e - hi.astype(F32)
        mid = r1.astype(BF16)
        lo = (r1 - mid.astype(F32)).astype(BF16)
        affs = sum(_dot(_nt(eye, part).astype(BF16), at) for part in (hi, mid, lo))
        val_o[0, e:e + 1, :] = jnp.sum(jnp.where(hit, affs, 0.0), axis=0, keepdims=True)


def _route_call(aff, bsz, n):
    cap = CAP_FACTOR * n // N_EXPERTS
    capl = -(-cap // LANES) * LANES
    nblk = max(n // LANES, 16)
    a = jnp.swapaxes(aff[:, :N_EXPERTS].reshape(bsz, n, N_EXPERTS), 1, 2)
    a = jnp.pad(a, ((0, 0), (0, 0), (0, nblk * LANES - n)), constant_values=-1.0)
    a = a.reshape(bsz, N_EXPERTS, nblk, LANES)
    rows = N_EXPERTS * nblk
    r = np.arange(rows)
    lblk = jnp.asarray((r[:, None] // nblk == r[None, :] // nblk) & (r[None, :] < r[:, None]), BF16)
    idx, val = pl.pallas_call(
        functools.partial(_route_body, cap=cap),
        grid=(bsz,),
        in_specs=[pl.BlockSpec((1, N_EXPERTS, nblk, LANES), lambda b: (b, 0, 0, 0)),
                  pl.BlockSpec((rows, rows), lambda b: (0, 0))],
        out_specs=[pl.BlockSpec((1, N_EXPERTS, capl), lambda b: (b, 0, 0))] * 2,
        out_shape=[jax.ShapeDtypeStruct((bsz, N_EXPERTS, capl), jnp.int32),
                   jax.ShapeDtypeStruct((bsz, N_EXPERTS, capl), F32)],
        compiler_params=_cparams(("arbitrary",)),
        name="route",
    )(a, lblk)
    return (idx[:, :, :cap].reshape(bsz * N_EXPERTS, 1, cap),
            val[:, :, :cap].reshape(bsz * N_EXPERTS, 1, cap))


def _moe_body(row_cur, row_prev, nxt_ref, val_cur, val_prev, h_hbm, wgu_ref, wd_ref, out_hbm,
              buf0, buf1, ybuf0, ybuf1, acc, sem0, sem1, osem, *, ts, nt, n_tok, n_exp):
    be = pl.program_id(0)
    t = pl.program_id(1)
    b = be // n_exp
    e = be % n_exp
    step = be * nt + t
    total = pl.num_programs(0) * nt
    rows = ts * SUBLANES
    nrows = n_tok * SUBLANES
    groups = ts // SUBLANES

    def whole(buf_ref, sem_ref):
        return pltpu.make_async_copy(h_hbm.at[pl.ds(0, rows), :], buf_ref, sem_ref)

    def zero_acc():
        zrows = min(nrows, 4096)

        def zero(i, carry):
            acc[pl.ds(pl.multiple_of(i * zrows, zrows), zrows), :] = jnp.zeros((zrows, LANES), F32)
            return carry
        lax.fori_loop(0, nrows // zrows, zero, 0)

    def flush(sample):
        cp = pltpu.make_async_copy(
            acc, out_hbm.at[pl.ds(pl.multiple_of(sample * nrows, nrows), nrows), :], osem)
        cp.start()
        cp.wait()

    def scatter_group(yb, row_ref, val_ref, g):
        upd = []
        for s in range(SUBLANES):
            j = g * SUBLANES + s
            r = pl.multiple_of(row_ref[0, 0, j], SUBLANES)
            yrow = yb[g, pl.ds(s, SUBLANES, stride=SUBLANES), :]
            upd.append((r, acc[pl.ds(r, SUBLANES), :] + yrow * val_ref[0, 0, j]))
        for r, v in upd:
            acc[pl.ds(r, SUBLANES), :] = v

    @pl.when(step == 0)
    def _():
        def grp(g, carry):
            for s in range(SUBLANES):
                j = g * SUBLANES + s
                r = pl.multiple_of(row_cur[0, 0, j] + b * nrows, SUBLANES)
                pltpu.make_async_copy(
                    h_hbm.at[pl.ds(r, SUBLANES), :],
                    buf0.at[pl.ds(pl.multiple_of(j * SUBLANES, SUBLANES), SUBLANES), :], sem0).start()
            return carry
        lax.fori_loop(0, groups, grp, 0)
        zero_acc()
        ybuf1[...] = jnp.zeros(ybuf1.shape, F32)

    def run(src, ssem, dst, dsem, ycur, yprev):
        whole(src, ssem).wait()
        for j in range(ts):
            pltpu.make_async_copy(
                h_hbm.at[pl.ds(pl.multiple_of(nxt_ref[0, 0, j], SUBLANES), SUBLANES), :],
                dst.at[pl.ds(j * SUBLANES, SUBLANES), :], dsem).start()

        for g in range(groups):
            scatter_group(yprev, row_prev, val_prev, g)

        d = wgu_ref.shape[1]
        x = jnp.concatenate(
            [src[pl.ds(s, ts, stride=SUBLANES), :] for s in range(d // LANES)], axis=1).astype(BF16)
        au = _dot(x, wgu_ref[0])
        dff = au.shape[1] // 2
        a = au[:, :dff]
        hmid = (a * (1.0 / (1.0 + jnp.exp(-a))) * au[:, dff:]).astype(BF16)
        y = _dot(hmid, wd_ref[0])
        for c in range(d // LANES):
            ycur[:, c * SUBLANES:(c + 1) * SUBLANES, :] = y[:, c * LANES:(c + 1) * LANES].reshape(
                groups, SUBLANES, LANES)

        @pl.when(step == total - 1)
        def _():
            whole(dst, dsem).wait()

            def last(g, carry):
                scatter_group(ycur, row_cur, val_cur, g)
                return carry
            lax.fori_loop(0, groups, last, 0)

    @pl.when(step % 2 == 0)
    def _():
        run(buf0, sem0, buf1, sem1, ybuf0, ybuf1)

    @pl.when(step % 2 == 1)
    def _():
        run(buf1, sem1, buf0, sem0, ybuf1, ybuf0)

    @pl.when(jnp.logical_and(jnp.logical_and(e == 0, t == 0), step > 0))
    def _():
        flush(b - 1)
        zero_acc()

    @pl.when(step == total - 1)
    def _():
        flush(b)


def _moe_call(idx3, val3, h8, wgu, wd, ts, n_tok):
    nbe, _, cap = idx3.shape
    nt = cap // ts
    e = wgu.shape[0]
    tiles = lambda a: a.reshape(nbe * nt, 1, ts)
    row_cur = tiles(idx3 * SUBLANES)
    row_prev = jnp.roll(row_cur, 1, axis=0)
    val_cur = tiles(val3)
    val_prev = jnp.roll(val_cur, 1, axis=0).at[0].set(0.0)
    tok0 = (jnp.arange(nbe, dtype=jnp.int32) // e * n_tok)[:, None, None]
    nxt = jnp.roll(tiles((idx3 + tok0) * SUBLANES), -1, axis=0)
    smem = pl.BlockSpec((1, 1, ts), lambda be, t: (be * nt + t, 0, 0), memory_space=pltpu.SMEM)
    ybuf = pltpu.VMEM((ts // SUBLANES, wgu.shape[1] // LANES * SUBLANES, LANES), F32)
    return pl.pallas_call(
        functools.partial(_moe_body, ts=ts, nt=nt, n_tok=n_tok, n_exp=e),
        grid=(nbe, nt),
        in_specs=[
            smem, smem, smem, smem, smem,
            pl.BlockSpec(memory_space=pl.ANY),
            pl.BlockSpec((1,) + wgu.shape[1:], lambda be, t: (be % e, 0, 0)),
            pl.BlockSpec((1,) + wd.shape[1:], lambda be, t: (be % e, 0, 0)),
        ],
        out_specs=pl.BlockSpec(memory_space=pl.ANY),
        out_shape=jax.ShapeDtypeStruct(h8.shape, F32),
        scratch_shapes=[
            pltpu.VMEM((ts * SUBLANES, LANES), F32),
            pltpu.VMEM((ts * SUBLANES, LANES), F32),
            ybuf, ybuf,
            pltpu.VMEM((n_tok * SUBLANES, LANES), F32),
            pltpu.SemaphoreType.DMA(()),
            pltpu.SemaphoreType.DMA(()),
            pltpu.SemaphoreType.DMA(()),
        ],
        compiler_params=pltpu.CompilerParams(dimension_semantics=("arbitrary", "arbitrary"),
                                             vmem_limit_bytes=MOE_VMEM_LIMIT),
        name="experts",
    )(row_cur, row_prev, nxt, val_cur, val_prev, h8, wgu, wd)


def _resid_body(x_ref, y_ref, mod_ref, nw_ref, o_ref, *, final):
    tm, d = x_ref.shape
    y = jnp.concatenate(
        [y_ref[pl.ds(s, tm, stride=SUBLANES), :] for s in range(d // LANES)], axis=1)
    x = x_ref[...] + mod_ref[0][:, 5 * d:6 * d] * y
    o_ref[...] = _rms(x, nw_ref[...]) if final else x


def _resid_call(x2, y8, mod3, mod_row_fn, nw, tm, final):
    t, d = x2.shape
    row = pl.BlockSpec((tm, d), lambda i: (i, 0))
    return pl.pallas_call(
        functools.partial(_resid_body, final=final),
        grid=(t // tm,),
        in_specs=[row, pl.BlockSpec((tm * SUBLANES, LANES), lambda i: (i, 0)),
                  pl.BlockSpec((1, 1, mod3.shape[2]), lambda i: (mod_row_fn(i), 0, 0)),
                  pl.BlockSpec(nw.shape, lambda i: (0, 0))],
        out_specs=row,
        out_shape=jax.ShapeDtypeStruct((t, d), F32),
        compiler_params=_cparams(("arbitrary",)),
        name="resid",
    )(x2, y8, mod3, nw)


def _rope_tables(n):
    pos_r = (np.arange(n) // GRID_W).astype(np.float32)
    pos_c = (np.arange(n) % GRID_W).astype(np.float32)

    def pattern(nf):
        inv = ROPE_BASE ** (-np.arange(nf, dtype=np.float32) / nf)
        ar = pos_r[:, None] * inv[None, :]
        ac = pos_c[:, None] * inv[None, :]
        cos = np.concatenate([np.cos(ar), np.cos(ar), np.cos(ac), np.cos(ac)], axis=1)
        sin = np.concatenate([-np.sin(ar), np.sin(ar), -np.sin(ac), np.sin(ac)], axis=1)
        return cos.astype(np.float32), sin.astype(np.float32)

    ca, sa = pattern(DH_A // 4)
    cb, sb = pattern(ROPE_B // 4)
    ca2, sa2 = np.tile(ca, (1, 2)), np.tile(sa, (1, 2))
    one64 = np.ones((n, NOPE_B), np.float32)
    z64 = np.zeros((n, NOPE_B), np.float32)
    z32 = np.zeros((n, LANES - NOPE_B - ROPE_B), np.float32)
    cbp = np.concatenate([one64, cb, z32], axis=1)
    sbp = np.concatenate([z64, sb, z32], axis=1)
    sc_a = DH_A ** -0.5 * LOG2E
    sc_b = (NOPE_B + ROPE_B) ** -0.5 * LOG2E
    tabs = [ca2 * sc_a, sa2 * sc_a, ca2, sa2, cbp * sc_b, sbp * sc_b, cbp, sbp]
    return [jnp.asarray(t, F32) for t in tabs]


def _ident_tables(n):
    one = np.ones((n, LANES), np.float32)
    zero = np.zeros((n, LANES), np.float32)
    z32 = np.zeros((n, LANES - NOPE_B - ROPE_B), np.float32)
    onep = np.concatenate([np.ones((n, NOPE_B + ROPE_B), np.float32), z32], axis=1)
    sc_a = DH_A ** -0.5 * LOG2E
    sc_b = (NOPE_B + ROPE_B) ** -0.5 * LOG2E
    tabs = [one * sc_a, zero, one, zero, onep * sc_b, zero, onep, zero]
    return [jnp.asarray(t, F32) for t in tabs]


def _pair_perm():
    cols = []
    for p in range(G_A):
        for h in (p, G_A + p):
            cols.extend(range(h * DH_A, (h + 1) * DH_A))
    return np.asarray(cols)


def _prep_layer(w_in, w_q_up, w_kv_up, w_out, w_router, w_gate, w_up, w_down):
    d = w_in.shape[0]
    perm = _pair_perm()
    o_ka = H_A * DH_A
    o_va = o_ka + KV_A * DH_A
    o_cq = o_va + KV_A * DH_A
    o_ckv = o_cq + Q_RANK
    o_kr = o_ckv + KV_RANK
    z = lambda w: jnp.zeros((d, w), F32)
    win = jnp.concatenate([
        w_in[:, :o_ka][:, perm], w_in[:, o_ka:o_va],
        w_in[:, o_va:o_va + DH_A], z(LANES - DH_A), w_in[:, o_va + DH_A:o_cq], z(LANES - DH_A),
        w_in[:, o_cq:o_ckv], w_in[:, o_ckv:o_kr],
        z(NOPE_B), w_in[:, o_kr:], z(LANES - NOPE_B - ROPE_B)], axis=1).astype(BF16)
    wq = w_q_up.reshape(Q_RANK, H_B, NOPE_B + ROPE_B)
    wq = jnp.pad(wq, ((0, 0), (0, 0), (0, LANES - NOPE_B - ROPE_B))).reshape(Q_RANK, H_B * LANES)
    wkv = w_kv_up.reshape(KV_RANK, H_B, NOPE_B + V_B)
    wk = jnp.pad(wkv[:, :, :NOPE_B], ((0, 0), (0, 0), (0, LANES - NOPE_B))).reshape(KV_RANK, H_B * LANES)
    wv = jnp.pad(wkv[:, :, NOPE_B:], ((0, 0), (0, 0), (0, LANES - V_B))).reshape(KV_RANK, H_B * LANES)
    wkv_p = jnp.concatenate([wk, wv], axis=1)
    wa = w_out[:H_A * DH_A][perm]
    wb = w_out[H_A * DH_A:]
    wr = jnp.pad(w_router, ((0, 0), (0, LANES - N_EXPERTS)))
    r_hi = wr.astype(BF16)
    r_lo = (wr - r_hi.astype(F32)).astype(BF16)
    wgu = jnp.concatenate([w_gate, w_up], axis=2).astype(BF16)
    return dict(win=win, wq=wq.astype(BF16), wkv=wkv_p.astype(BF16), wa=wa.astype(BF16),
                wb=wb.astype(BF16), r_hi=r_hi, r_lo=r_lo, wgu=wgu, wd=w_down.astype(BF16))


def _tile(n, pref):
    t = pref
    while n % t:
        t //= 2
    return t


def kernel(x, c, ctx, c_ctx, w_mod, b_mod, norm_attn, norm_ffn, w_in, sink, q_norm, kv_norm,
           w_q_up, w_kv_up, w_out, w_router, w_gate, w_up, w_down, norm_final):
    bsz, n, d = x.shape
    lc = ctx.shape[1]
    depth = w_mod.shape[0]
    assert bsz < SUBLANES and n % 256 == 0 and n >= 512 and lc % LANES == 0
    ctx_row = bsz

    cs = jnp.zeros((SUBLANES, d), F32).at[:bsz].set(c).at[ctx_row].set(c_ctx)
    mod = _mod_call(cs, w_mod, b_mod)
    tabs_lat = _rope_tables(n)
    tabs_ctx = _ident_tables(lc)

    tm = _tile(n, 512)
    tq_w = 256
    tq_m = _tile(n, 1024)
    tk_m = _tile(n, 2048)
    cap = CAP_FACTOR * n // N_EXPERTS
    cap_c = CAP_FACTOR * lc // N_EXPERTS
    ts = _tile(cap, 512)

    xl = x.reshape(bsz * n, d)
    xc = ctx.reshape(bsz * lc, d)
    row2 = lambda a: a.reshape(1, -1)

    for l in range(depth):
        last = l == depth - 1
        w = _prep_layer(w_in[l], w_q_up[l], w_kv_up[l], w_out[l], w_router[l],
                        w_gate[l], w_up[l], w_down[l])
        mod3 = mod[l].reshape(SUBLANES, 1, 6 * d)
        lat_row = lambda i: i // (n // tm)
        ctx_rowf = lambda i: ctx_row
        na, nf = row2(norm_attn[l]), row2(norm_ffn[l])
        qn, kvn = row2(q_norm[l]), row2(kv_norm[l])

        qa, ka, va, qb, kb, vb = _proj_call(xl, mod3, lat_row, na, w["win"], qn, kvn,
                                            w["wq"], w["wkv"], tabs_lat, tm, n // tm)
        qa_c, ka_c, va_c, qb_c, kb_c, vb_c = _proj_call(xc, mod3, ctx_rowf, na, w["win"], qn, kvn,
                                                        w["wq"], w["wkv"], tabs_ctx, lc, 1)
        o_a = _win_call(sink[l], qa, ka_c, va_c, ka, va, bsz, n, lc, tq_w)
        o_b = _mla_call(qb, kb_c, vb_c, kb, vb, bsz, n, lc, tq_m, tk_m)
        xl, h8, aff = _outp_call(o_a, o_b, xl, mod3, lat_row, w["wa"], w["wb"], nf,
                                 w["r_hi"], w["r_lo"], tm)
        idx3, val3 = _route_call(aff, bsz, n)
        moe8 = _moe_call(idx3, val3, h8, w["wgu"], w["wd"], ts, n)
        xl = _resid_call(xl, moe8, mod3, lat_row, row2(norm_final), tm, last)

        if not last:
            o_a_c = _win_call(sink[l], qa_c, ka_c, va_c, None, None, bsz, lc, lc, lc)
            o_b_c = _mla_call(qb_c, kb_c, vb_c, None, None, bsz, lc, lc, lc, lc)
            xc, h8c, aff_c = _outp_call(o_a_c, o_b_c, xc, mod3, ctx_rowf, w["wa"], w["wb"], nf,
                                        w["r_hi"], w["r_lo"], lc)
            idx3c, val3c = _route_call(aff_c, bsz, lc)
            moe8c = _moe_call(idx3c, val3c, h8c, w["wgu"], w["wd"], cap_c, lc)
            xc = _resid_call(xc, moe8c, mod3, ctx_rowf, row2(norm_final), lc, False)

    return xl.reshape(bsz, n, d)
```

```python
import functools
import math

import jax
import jax.numpy as jnp
import numpy as np
from jax import lax
from jax.experimental import pallas as pl
from jax.experimental.pallas import tpu as pltpu

F32 = jnp.float32
BF16 = jnp.bfloat16

GRID_W = 64
H_A, KV_A, DH_A = 8, 2, 64
G_A = H_A // KV_A
WINDOW = 128
H_B, NOPE_B, ROPE_B, V_B = 8, 64, 32, 64
Q_RANK, KV_RANK = 384, 256
N_EXPERTS, CAP_FACTOR, D_FF = 16, 2, 512
ROPE_BASE = 10000.0
EPS = 1e-6
LOG2E = math.log2(math.e)
LANES = 128
SUBLANES = 8
NEG_INF = float("-inf")

C_QA, C_KA, C_VA, C_CQ, C_CKV, C_KR, C_END = 0, 512, 640, 896, 1280, 1536, 1664
VMEM_LIMIT = 56 * 1024 * 1024
MOE_VMEM_LIMIT = 60 * 1024 * 1024


def _cparams(sem):
    return pltpu.CompilerParams(dimension_semantics=sem, vmem_limit_bytes=VMEM_LIMIT)


def _nt(a, b):
    return lax.dot_general(a, b, (((1,), (1,)), ((), ())), preferred_element_type=F32)


def _dot(a, b):
    return jnp.dot(a, b, preferred_element_type=F32)


def _split(a):
    hi = a.astype(BF16)
    lo = (a - hi.astype(F32)).astype(BF16)
    return hi, lo


def _dot3(a, b_hi, b_lo):
    a_hi, a_lo = _split(a)
    return _dot(a_hi, b_hi) + (_dot(a_hi, b_lo) + _dot(a_lo, b_hi))


def _lane_tile(t, reps):
    return t if reps == 1 else jnp.concatenate([t] * reps, axis=1)


def _rope(x, cos, sin, half):
    w = x.shape[1]
    reps = w // LANES
    lane = lax.broadcasted_iota(jnp.int32, x.shape, 1)
    first = (lane & (2 * half - 1)) < half
    partner = jnp.where(first, pltpu.roll(x, w - half, 1), pltpu.roll(x, half, 1))
    return x * _lane_tile(cos, reps) + partner * _lane_tile(sin, reps)


def _rms(x, w):
    ms = jnp.mean(x * x, axis=-1, keepdims=True)
    return x * lax.rsqrt(ms + EPS) * w


def _mod_body(c_ref, w_ref, b_ref, o_ref):
    c = c_ref[...]
    s = c * (1.0 / (1.0 + jnp.exp(-c)))
    w_hi, w_lo = _split(w_ref[0])
    o_ref[0] = _dot3(s, w_hi, w_lo) + b_ref[0]


def _mod_call(cs, w_mod, b_mod):
    depth, d, d6 = w_mod.shape
    tn = 1536
    return pl.pallas_call(
        _mod_body,
        grid=(depth, d6 // tn),
        in_specs=[
            pl.BlockSpec((SUBLANES, d), lambda l, j: (0, 0)),
            pl.BlockSpec((1, d, tn), lambda l, j: (l, 0, j)),
            pl.BlockSpec((1, 1, tn), lambda l, j: (l, 0, j)),
        ],
        out_specs=pl.BlockSpec((1, SUBLANES, tn), lambda l, j: (l, 0, j)),
        out_shape=jax.ShapeDtypeStruct((depth, SUBLANES, d6), F32),
        compiler_params=_cparams(("arbitrary", "arbitrary")),
        name="mod",
    )(cs, w_mod, b_mod.reshape(depth, 1, d6))


def _proj_body(x_ref, mod_ref, nw_ref, win_ref, qn_ref, kvn_ref, wq_ref, wkv_ref,
               caq, saq, cak, sak, cbq, sbq, cbk, sbk,
               qa_o, ka_o, va_o, qb_o, kb_o, vb_o):
    d = x_ref.shape[1]
    m = mod_ref[0]
    y = _rms(x_ref[...], nw_ref[...])
    h = (y * (1.0 + m[:, d:2 * d]) + m[:, 0:d]).astype(BF16)
    p = _dot(h, win_ref[...])
    qa_o[...] = _rope(p[:, C_QA:C_KA], caq[...], saq[...], 16).astype(BF16)
    ka_o[...] = _rope(p[:, C_KA:C_VA], cak[...], sak[...], 16).astype(BF16)
    alane = lax.broadcasted_iota(jnp.int32, (x_ref.shape[0], C_CQ - C_VA), 1)
    va_o[...] = (p[:, C_VA:C_CQ] + jnp.where((alane & (LANES - 1)) >= DH_A, 1.0, 0.0)).astype(BF16)
    cq = _rms(p[:, C_CQ:C_CKV], qn_ref[...]).astype(BF16)
    qb = _dot(cq, wq_ref[...])
    qb_o[...] = _rope(qb, cbq[...], sbq[...], 8).astype(BF16)
    ckv = _rms(p[:, C_CKV:C_KR], kvn_ref[...]).astype(BF16)
    kvb = _dot(ckv, wkv_ref[...])
    kr = _rope(p[:, C_KR:C_END], cbk[...], sbk[...], 8)
    kw = H_B * LANES
    kb_o[...] = (kvb[:, 0:kw] + _lane_tile(kr, H_B)).astype(BF16)
    vlane = lax.broadcasted_iota(jnp.int32, (x_ref.shape[0], kw), 1)
    ones_pad = jnp.where((vlane & (LANES - 1)) >= V_B, 1.0, 0.0)
    vb_o[...] = (kvb[:, kw:] + ones_pad).astype(BF16)


def _proj_call(x2, mod3, mod_row_fn, nw, win, qn, kvn, wq, wkv, tabs, tm, tab_blocks):
    t, d = x2.shape
    full = lambda a: pl.BlockSpec(a.shape, lambda i: (0,) * a.ndim)
    tab_spec = pl.BlockSpec((tm, LANES), lambda i: (i % tab_blocks, 0))
    widths = (512, 128, 256, 1024, 1024, 1024)
    return pl.pallas_call(
        _proj_body,
        grid=(t // tm,),
        in_specs=[
            pl.BlockSpec((tm, d), lambda i: (i, 0)),
            pl.BlockSpec((1, 1, mod3.shape[2]), lambda i: (mod_row_fn(i), 0, 0)),
            full(nw), full(win), full(qn), full(kvn), full(wq), full(wkv),
        ] + [tab_spec] * 8,
        out_specs=[pl.BlockSpec((tm, w), lambda i: (i, 0)) for w in widths],
        out_shape=[jax.ShapeDtypeStruct((t, w), BF16) for w in widths],
        compiler_params=_cparams(("arbitrary",)),
        name="proj",
    )(x2, mod3, nw, win, qn, kvn, wq, wkv, *tabs)


def _win_body(sink_ref, q_ref, kc_ref, vc_ref, *rest, n, tq, band):
    if band:
        k_ref, v_ref, o_ref = rest
    else:
        (o_ref,) = rest
    i = pl.program_id(1)
    kc = kc_ref[...]
    if band:
        wb = tq + 2 * WINDOW
        start = jnp.clip(i * tq - WINDOW, 0, n - wb)
        start = pl.multiple_of(start, LANES)
        kb = k_ref[pl.ds(start, wb), :]
        qpos = i * tq + lax.broadcasted_iota(jnp.int32, (tq, wb), 0)
        kpos = start + lax.broadcasted_iota(jnp.int32, (tq, wb), 1)
        valid = jnp.abs(qpos - kpos) <= WINDOW
    lo = lax.broadcasted_iota(jnp.int32, (tq, LANES), 1) < DH_A
    zero = jnp.zeros((tq, LANES), BF16)
    o_g = []
    for g in range(KV_A):
        sel = lo if g == 0 else jnp.logical_not(lo)
        qg = jnp.concatenate(
            [jnp.where(sel, q_ref[:, p * LANES:(p + 1) * LANES], zero) for p in range(G_A)], axis=0)
        gs = slice(g * LANES, (g + 1) * LANES)
        vc = vc_ref[:, gs]
        s_c = _nt(qg, kc)
        if band:
            s_b = _nt(qg, kb)
            vb = v_ref[pl.ds(start, wb), gs]
        o_p = []
        for p in range(G_A):
            h = p + g * G_A
            sink2 = sink_ref[h] * LOG2E
            sc = s_c[p * tq:(p + 1) * tq]
            m = jnp.maximum(jnp.max(sc, axis=-1, keepdims=True), sink2)
            if band:
                sb = jnp.where(valid, s_b[p * tq:(p + 1) * tq], NEG_INF)
                m = jnp.maximum(m, jnp.max(sb, axis=-1, keepdims=True))
            o = _dot(jnp.exp2((sc - m).astype(BF16)), vc)
            if band:
                o = o + _dot(jnp.exp2((sb - m).astype(BF16)), vb)
            den = pltpu.roll(o, DH_A, 1) + jnp.exp2(sink2 - m)
            o_p.append(o * (1.0 / den))
        o_g.append(o_p)
    for p in range(G_A):
        o_ref[:, p * LANES:(p + 1) * LANES] = jnp.where(
            lo, o_g[0][p], pltpu.roll(o_g[1][p], DH_A, 1)).astype(BF16)


def _win_call(sink, q, kc, vc, k, v, bsz, n, lc, tq):
    band = k is not None
    nq = n // tq
    in_specs = [
        pl.BlockSpec(memory_space=pltpu.SMEM),
        pl.BlockSpec((tq, H_A * DH_A), lambda b, i: (b * nq + i, 0)),
        pl.BlockSpec((lc, LANES), lambda b, i: (b, 0)),
        pl.BlockSpec((lc, KV_A * LANES), lambda b, i: (b, 0)),
    ]
    args = [sink, q, kc, vc]
    if band:
        in_specs += [pl.BlockSpec((n, LANES), lambda b, i: (b, 0)),
                     pl.BlockSpec((n, KV_A * LANES), lambda b, i: (b, 0))]
        args += [k, v]
    return pl.pallas_call(
        functools.partial(_win_body, n=n, tq=tq, band=band),
        grid=(bsz, nq),
        in_specs=in_specs,
        out_specs=pl.BlockSpec((tq, H_A * DH_A), lambda b, i: (b * nq + i, 0)),
        out_shape=jax.ShapeDtypeStruct((bsz * n, H_A * DH_A), BF16),
        compiler_params=_cparams(("arbitrary", "arbitrary")),
        name="win_lat" if band else "win_ctx",
    )(*args)


def _mla_body(q_ref, kc_ref, vc_ref, *rest, n, tk, has_lat):
    if has_lat:
        k_ref, v_ref, o_ref, m_sc, acc_sc = rest
    else:
        (o_ref,) = rest
    tq = q_ref.shape[0]
    lo = lax.broadcasted_iota(jnp.int32, (tq, LANES), 1) < V_B

    def lane_max(p):
        parts = [p[:, c * LANES:(c + 1) * LANES] for c in range(p.shape[1] // LANES)]
        while len(parts) > 1:
            parts = [jnp.maximum(a, b) for a, b in zip(parts[0::2], parts[1::2])] + (
                [parts[-1]] if len(parts) % 2 else [])
        return parts[0]

    def row_max_rep(p):
        r = jnp.max(lane_max(p), axis=-1, keepdims=True)
        return jnp.broadcast_to(r, (tq, LANES))

    def probs(s, m):
        return jnp.exp2((s - _lane_tile(m, s.shape[1] // LANES)).astype(BF16))

    qs, ms, accs = [], [], []
    for hh in range(2):
        hs = slice(hh * LANES, (hh + 1) * LANES)
        q = q_ref[:, hs]
        s = _nt(q, kc_ref[:, hs])
        m = row_max_rep(s)
        qs.append(q)
        ms.append(m)
        accs.append(_dot(probs(s, m), vc_ref[:, hs]))

    if has_lat:
        for hh in range(2):
            m_sc[hh] = ms[hh]
            acc_sc[hh] = accs[hh]

        def chunk(c, carry):
            off = pl.multiple_of(c * tk, tk)
            for hh in range(2):
                hs = slice(hh * LANES, (hh + 1) * LANES)
                s = _nt(qs[hh], k_ref[pl.ds(off, tk), hs])
                m_old = m_sc[hh]
                m_new = jnp.maximum(m_old, row_max_rep(s))
                alpha = jnp.exp2(m_old - m_new)
                m_sc[hh] = m_new
                acc_sc[hh] = alpha * acc_sc[hh] + _dot(probs(s, m_new), v_ref[pl.ds(off, tk), hs])
            return carry

        lax.fori_loop(0, n // tk, chunk, 0)
        accs = [acc_sc[0], acc_sc[1]]

    outs = [a * pltpu.roll(1.0 / a, V_B, 1) for a in accs]
    o_ref[...] = jnp.where(lo, outs[0], pltpu.roll(outs[1], V_B, 1)).astype(BF16)


def _mla_call(q, kc, vc, k, v, bsz, n, lc, tq, tk):
    has_lat = k is not None
    nq = n // tq
    npair = H_B // 2
    in_specs = [
        pl.BlockSpec((tq, 2 * LANES), lambda b, h, i: (b * nq + i, h)),
        pl.BlockSpec((lc, 2 * LANES), lambda b, h, i: (b, h)),
        pl.BlockSpec((lc, 2 * LANES), lambda b, h, i: (b, h)),
    ]
    args = [q, kc, vc]
    scratch = []
    if has_lat:
        in_specs += [pl.BlockSpec((n, 2 * LANES), lambda b, h, i: (b, h))] * 2
        args += [k, v]
        scratch = [pltpu.VMEM((2, tq, LANES), F32)] * 2
    return pl.pallas_call(
        functools.partial(_mla_body, n=n, tk=tk, has_lat=has_lat),
        grid=(bsz, npair, nq),
        in_specs=in_specs,
        out_specs=pl.BlockSpec((tq, LANES), lambda b, h, i: (b * nq + i, h)),
        out_shape=jax.ShapeDtypeStruct((bsz * n, H_B * V_B), BF16),
        scratch_shapes=scratch,
        compiler_params=_cparams(("arbitrary", "arbitrary", "arbitrary")),
        name="mla_lat" if has_lat else "mla_ctx",
    )(*args)


def _outp_body(oa_ref, ob_ref, x_ref, mod_ref, wa_ref, wb_ref, nw_ref, rhi_ref, rlo_ref,
               x_o, h_o, aff_o):
    d = x_ref.shape[1]
    tm = x_ref.shape[0]
    m = mod_ref[0]
    mix = _dot(oa_ref[...], wa_ref[...]) + _dot(ob_ref[...], wb_ref[...])
    x = x_ref[...] + m[:, 2 * d:3 * d] * mix
    x_o[...] = x
    h = _rms(x, nw_ref[...]) * (1.0 + m[:, 4 * d:5 * d]) + m[:, 3 * d:4 * d]
    for s in range(d // LANES):
        h_o[pl.ds(s, tm, stride=SUBLANES), :] = h[:, s * LANES:(s + 1) * LANES]
    logits = _dot3(h, rhi_ref[...], rlo_ref[...])
    lane = lax.broadcasted_iota(jnp.int32, logits.shape, 1)
    logits = jnp.where(lane < N_EXPERTS, logits, NEG_INF)
    e = jnp.exp(logits - jnp.max(logits, axis=-1, keepdims=True))
    aff_o[...] = e / jnp.sum(e, axis=-1, keepdims=True)


def _outp_call(oa, ob, x2, mod3, mod_row_fn, wa, wb, nw, rhi, rlo, tm):
    t, d = x2.shape
    full = lambda a: pl.BlockSpec(a.shape, lambda i: (0,) * a.ndim)
    row = lambda w: pl.BlockSpec((tm, w), lambda i: (i, 0))
    return pl.pallas_call(
        _outp_body,
        grid=(t // tm,),
        in_specs=[row(oa.shape[1]), row(ob.shape[1]), row(d),
                  pl.BlockSpec((1, 1, mod3.shape[2]), lambda i: (mod_row_fn(i), 0, 0)),
                  full(wa), full(wb), full(nw), full(rhi), full(rlo)],
        out_specs=[row(d), pl.BlockSpec((tm * SUBLANES, LANES), lambda i: (i, 0)), row(LANES)],
        out_shape=[jax.ShapeDtypeStruct((t, d), F32),
                   jax.ShapeDtypeStruct((t * SUBLANES, LANES), F32),
                   jax.ShapeDtypeStruct((t, LANES), F32)],
        compiler_params=_cparams(("arbitrary",)),
        name="outp",
    )(oa, ob, x2, mod3, wa, wb, nw, rhi, rlo)


def _route_body(a_ref, lblk_ref, idx_o, val_o, *, cap):
    n_exp, nblk = a_ref.shape[1], a_ref.shape[2]
    capl = idx_o.shape[2]
    rows = n_exp * nblk
    a3 = a_ref[0]
    a2 = a3.reshape(rows, LANES)

    ri = lax.broadcasted_iota(jnp.int32, (LANES, LANES), 0)
    ci = lax.broadcasted_iota(jnp.int32, (LANES, LANES), 1)
    upper = jnp.where(ri <= ci, 1.0, 0.0).astype(BF16)
    eye = jnp.where(ri == ci, 1.0, 0.0).astype(BF16)
    ones = jnp.ones((LANES, LANES), BF16)
    lblk = lblk_ref[...]

    def as01(mask3):
        return jnp.where(mask3, 1.0, 0.0).reshape(rows, LANES).astype(BF16)

    def red(x, op):
        return op(op(x, axis=1, keepdims=True), axis=2, keepdims=True)

    def count(mask3):
        return red(jnp.where(mask3, 1.0, 0.0), jnp.sum)

    def prefix(m01):
        within = _dot(m01, upper)
        tot = _dot(m01, ones)
        off = _dot(lblk, tot.astype(BF16))
        return within, tot, off

    real = a3 >= 0.0
    big = jnp.float32(3.0e38)

    def undecided(state):
        it, lo, hi = state
        return jnp.logical_and(it < 2048, jnp.sum(jnp.where(lo < hi, 1.0, 0.0)) > 0.0)

    def bisect(state):
        it, lo, hi = state
        mid = lo + (hi - lo) * 0.5
        mid = jnp.where(mid >= hi, lo, mid)
        above = a3 > mid
        up = count(above) >= cap
        lo_up = red(jnp.where(above, a3, big), jnp.min)
        hi_dn = red(jnp.where(jnp.logical_or(above, jnp.logical_not(real)), -big, a3), jnp.max)
        return it + 1, jnp.where(up, lo_up, lo), jnp.where(up, hi, hi_dn)

    lo0 = red(jnp.where(real, a3, big), jnp.min)
    hi0 = red(a3, jnp.max)
    _, thr, _ = lax.while_loop(undecided, bisect, (jnp.int32(0), lo0, hi0))

    gt = a3 > thr
    eq = a3 == thr
    need = cap - count(gt)
    eq01 = as01(eq)
    w_eq, _, off_eq = prefix(eq01)
    rank_eq = (off_eq + w_eq - eq01.astype(F32)).reshape(n_exp, nblk, LANES)
    sel01 = as01(jnp.logical_or(gt, jnp.logical_and(eq, rank_eq < need)))
    within, tot, off = prefix(sel01)
    wsel = within * sel01.astype(F32)

    j = lax.broadcasted_iota(jnp.int32, (nblk, capl), 1).astype(F32)
    cbase = (lax.broadcasted_iota(jnp.int32, (nblk, capl), 0) * LANES).astype(F32)
    lidx = lax.broadcasted_iota(jnp.int32, (LANES, capl), 0).astype(F32)
    reps = capl // LANES
    for e in range(n_exp):
        rs = slice(e * nblk, (e + 1) * nblk)
        off_e = _lane_tile(off[rs], reps)
        in_blk = jnp.logical_and(j >= off_e, j < off_e + _lane_tile(tot[rs], reps))
        at = jnp.where(in_blk, 1.0, 0.0).astype(BF16)
        rj = jnp.sum(jnp.where(in_blk, j - off_e + 1.0, 0.0), axis=0, keepdims=True)
        base = jnp.sum(jnp.where(in_blk, cbase, 0.0), axis=0, keepdims=True)
        ranks = _dot(_nt(eye, wsel[rs].astype(BF16)).astype(BF16), at)
        hit = jnp.logical_and(ranks == rj, rj > 0.0)
        local = jnp.sum(jnp.where(hit, lidx, 0.0), axis=0, keepdims=True)
        idx_o[0, e:e + 1, :] = (base + local).astype(jnp.int32)
        a_e = a2[rs]
        hi = a_e.astype(BF16)
        r1 = a_e - hi.astype(F32)
        mid = r1.astype(BF16)
        lo = (r1 - mid.astype(F32)).astype(BF16)
        affs = sum(_dot(_nt(eye, part).astype(BF16), at) for part in (hi, mid, lo))
        val_o[0, e:e + 1, :] = jnp.sum(jnp.where(hit, affs, 0.0), axis=0, keepdims=True)


def _route_call(aff, bsz, n):
    cap = CAP_FACTOR * n // N_EXPERTS
    capl = -(-cap // LANES) * LANES
    nblk = max(n // LANES, 16)
    a = jnp.swapaxes(aff[:, :N_EXPERTS].reshape(bsz, n, N_EXPERTS), 1, 2)
    a = jnp.pad(a, ((0, 0), (0, 0), (0, nblk * LANES - n)), constant_values=-1.0)
    a = a.reshape(bsz, N_EXPERTS, nblk, LANES)
    rows = N_EXPERTS * nblk
    r = np.arange(rows)
    lblk = jnp.asarray((r[:, None] // nblk == r[None, :] // nblk) & (r[None, :] < r[:, None]), BF16)
    idx, val = pl.pallas_call(
        functools.partial(_route_body, cap=cap),
        grid=(bsz,),
        in_specs=[pl.BlockSpec((1, N_EXPERTS, nblk, LANES), lambda b: (b, 0, 0, 0)),
                  pl.BlockSpec((rows, rows), lambda b: (0, 0))],
        out_specs=[pl.BlockSpec((1, N_EXPERTS, capl), lambda b: (b, 0, 0))] * 2,
        out_shape=[jax.ShapeDtypeStruct((bsz, N_EXPERTS, capl), jnp.int32),
                   jax.ShapeDtypeStruct((bsz, N_EXPERTS, capl), F32)],
        compiler_params=_cparams(("arbitrary",)),
        name="route",
    )(a, lblk)
    return (idx[:, :, :cap].reshape(bsz * N_EXPERTS, 1, cap),
            val[:, :, :cap].reshape(bsz * N_EXPERTS, 1, cap))


def _moe_body(row_cur, nxt_ref, val_ref, h_hbm, wgu_ref, wd_ref, out_hbm,
              buf0, buf1, ybuf, acc, sem0, sem1, osem, *, ts, nt, n_tok, n_exp):
    be = pl.program_id(0)
    t = pl.program_id(1)
    b = be // n_exp
    e = be % n_exp
    step = be * nt + t
    total = pl.num_programs(0) * nt
    rows = ts * SUBLANES

    def whole(buf_ref, sem_ref):
        return pltpu.make_async_copy(h_hbm.at[pl.ds(0, rows), :], buf_ref, sem_ref)

    @pl.when(step == 0)
    def _():
        def grp(g, carry):
            for s in range(SUBLANES):
                j = g * SUBLANES + s
                r = pl.multiple_of(row_cur[0, 0, j] + b * (n_tok * SUBLANES), SUBLANES)
                pltpu.make_async_copy(
                    h_hbm.at[pl.ds(r, SUBLANES), :],
                    buf0.at[pl.ds(pl.multiple_of(j * SUBLANES, SUBLANES), SUBLANES), :], sem0).start()
            return carry
        lax.fori_loop(0, ts // SUBLANES, grp, 0)

    @pl.when(jnp.logical_and(e == 0, t == 0))
    def _():
        zrows = min(n_tok * SUBLANES, 4096)

        def zero(i, carry):
            acc[pl.ds(pl.multiple_of(i * zrows, zrows), zrows), :] = jnp.zeros((zrows, LANES), F32)
            return carry
        lax.fori_loop(0, n_tok * SUBLANES // zrows, zero, 0)

    def run(src, ssem, dst, dsem):
        whole(src, ssem).wait()
        for j in range(ts):
            pltpu.make_async_copy(
                h_hbm.at[pl.ds(pl.multiple_of(nxt_ref[0, 0, j], SUBLANES), SUBLANES), :],
                dst.at[pl.ds(j * SUBLANES, SUBLANES), :], dsem).start()

        d = wgu_ref.shape[1]
        x = jnp.concatenate(
            [src[pl.ds(s, ts, stride=SUBLANES), :] for s in range(d // LANES)], axis=1).astype(BF16)
        au = _dot(x, wgu_ref[0])
        dff = au.shape[1] // 2
        a = au[:, :dff]
        hmid = (a * (1.0 / (1.0 + jnp.exp(-a))) * au[:, dff:]).astype(BF16)
        y = _dot(hmid, wd_ref[0])
        for c in range(d // LANES):
            ybuf[:, c * SUBLANES:(c + 1) * SUBLANES, :] = y[:, c * LANES:(c + 1) * LANES].reshape(
                ts // SUBLANES, SUBLANES, LANES)

        def scatter(g, carry):
            upd = []
            for s in range(SUBLANES):
                j = t * ts + g * SUBLANES + s
                r = pl.multiple_of(row_cur[0, 0, j], SUBLANES)
                yrow = ybuf[g, pl.ds(s, SUBLANES, stride=SUBLANES), :]
                upd.append((r, acc[pl.ds(r, SUBLANES), :] + yrow * val_ref[0, 0, j]))
            for r, v in upd:
                acc[pl.ds(r, SUBLANES), :] = v
            return carry
        lax.fori_loop(0, ts // SUBLANES, scatter, 0)

        @pl.when(step == total - 1)
        def _():
            whole(dst, dsem).wait()

    @pl.when(step % 2 == 0)
    def _():
        run(buf0, sem0, buf1, sem1)

    @pl.when(step % 2 == 1)
    def _():
        run(buf1, sem1, buf0, sem0)

    @pl.when(jnp.logical_and(e == n_exp - 1, t == nt - 1))
    def _():
        nrows = n_tok * SUBLANES
        flush = pltpu.make_async_copy(
            acc, out_hbm.at[pl.ds(pl.multiple_of(b * nrows, nrows), nrows), :], osem)
        flush.start()
        flush.wait()


def _moe_call(idx3, val3, h8, wgu, wd, ts, n_tok):
    nbe, _, cap = idx3.shape
    nt = cap // ts
    e = wgu.shape[0]
    tok0 = (jnp.arange(nbe, dtype=jnp.int32) // e * n_tok)[:, None, None]
    nxt = jnp.roll(((idx3 + tok0) * SUBLANES).reshape(nbe * nt, 1, ts), -1, axis=0)
    smem = lambda imap: pl.BlockSpec((1, 1, cap), imap, memory_space=pltpu.SMEM)
    return pl.pallas_call(
        functools.partial(_moe_body, ts=ts, nt=nt, n_tok=n_tok, n_exp=e),
        grid=(nbe, nt),
        in_specs=[
            smem(lambda be, t: (be, 0, 0)),
            pl.BlockSpec((1, 1, ts), lambda be, t: (be * nt + t, 0, 0), memory_space=pltpu.SMEM),
            smem(lambda be, t: (be, 0, 0)),
            pl.BlockSpec(memory_space=pl.ANY),
            pl.BlockSpec((1,) + wgu.shape[1:], lambda be, t: (be % e, 0, 0)),
            pl.BlockSpec((1,) + wd.shape[1:], lambda be, t: (be % e, 0, 0)),
        ],
        out_specs=pl.BlockSpec(memory_space=pl.ANY),
        out_shape=jax.ShapeDtypeStruct(h8.shape, F32),
        scratch_shapes=[
            pltpu.VMEM((ts * SUBLANES, LANES), F32),
            pltpu.VMEM((ts * SUBLANES, LANES), F32),
            pltpu.VMEM((ts // SUBLANES, wgu.shape[1] // LANES * SUBLANES, LANES), F32),
            pltpu.VMEM((n_tok * SUBLANES, LANES), F32),
            pltpu.SemaphoreType.DMA(()),
            pltpu.SemaphoreType.DMA(()),
            pltpu.SemaphoreType.DMA(()),
        ],
        compiler_params=pltpu.CompilerParams(dimension_semantics=("arbitrary", "arbitrary"),
                                             vmem_limit_bytes=MOE_VMEM_LIMIT),
        name="experts",
    )(idx3 * SUBLANES, nxt, val3, h8, wgu, wd)


def _resid_body(x_ref, y_ref, mod_ref, nw_ref, o_ref, *, final):
    tm, d = x_ref.shape
    y = jnp.concatenate(
        [y_ref[pl.ds(s, tm, stride=SUBLANES), :] for s in range(d // LANES)], axis=1)
    x = x_ref[...] + mod_ref[0][:, 5 * d:6 * d] * y
    o_ref[...] = _rms(x, nw_ref[...]) if final else x


def _resid_call(x2, y8, mod3, mod_row_fn, nw, tm, final):
    t, d = x2.shape
    row = pl.BlockSpec((tm, d), lambda i: (i, 0))
    return pl.pallas_call(
        functools.partial(_resid_body, final=final),
        grid=(t // tm,),
        in_specs=[row, pl.BlockSpec((tm * SUBLANES, LANES), lambda i: (i, 0)),
                  pl.BlockSpec((1, 1, mod3.shape[2]), lambda i: (mod_row_fn(i), 0, 0)),
                  pl.BlockSpec(nw.shape, lambda i: (0, 0))],
        out_specs=row,
        out_shape=jax.ShapeDtypeStruct((t, d), F32),
        compiler_params=_cparams(("arbitrary",)),
        name="resid",
    )(x2, y8, mod3, nw)


def _rope_tables(n):
    pos_r = (np.arange(n) // GRID_W).astype(np.float32)
    pos_c = (np.arange(n) % GRID_W).astype(np.float32)

    def pattern(nf):
        inv = ROPE_BASE ** (-np.arange(nf, dtype=np.float32) / nf)
        ar = pos_r[:, None] * inv[None, :]
        ac = pos_c[:, None] * inv[None, :]
        cos = np.concatenate([np.cos(ar), np.cos(ar), np.cos(ac), np.cos(ac)], axis=1)
        sin = np.concatenate([-np.sin(ar), np.sin(ar), -np.sin(ac), np.sin(ac)], axis=1)
        return cos.astype(np.float32), sin.astype(np.float32)

    ca, sa = pattern(DH_A // 4)
    cb, sb = pattern(ROPE_B // 4)
    ca2, sa2 = np.tile(ca, (1, 2)), np.tile(sa, (1, 2))
    one64 = np.ones((n, NOPE_B), np.float32)
    z64 = np.zeros((n, NOPE_B), np.float32)
    z32 = np.zeros((n, LANES - NOPE_B - ROPE_B), np.float32)
    cbp = np.concatenate([one64, cb, z32], axis=1)
    sbp = np.concatenate([z64, sb, z32], axis=1)
    sc_a = DH_A ** -0.5 * LOG2E
    sc_b = (NOPE_B + ROPE_B) ** -0.5 * LOG2E
    tabs = [ca2 * sc_a, sa2 * sc_a, ca2, sa2, cbp * sc_b, sbp * sc_b, cbp, sbp]
    return [jnp.asarray(t, F32) for t in tabs]


def _ident_tables(n):
    one = np.ones((n, LANES), np.float32)
    zero = np.zeros((n, LANES), np.float32)
    z32 = np.zeros((n, LANES - NOPE_B - ROPE_B), np.float32)
    onep = np.concatenate([np.ones((n, NOPE_B + ROPE_B), np.float32), z32], axis=1)
    sc_a = DH_A ** -0.5 * LOG2E
    sc_b = (NOPE_B + ROPE_B) ** -0.5 * LOG2E
    tabs = [one * sc_a, zero, one, zero, onep * sc_b, zero, onep, zero]
    return [jnp.asarray(t, F32) for t in tabs]


def _pair_perm():
    cols = []
    for p in range(G_A):
        for h in (p, G_A + p):
            cols.extend(range(h * DH_A, (h + 1) * DH_A))
    return np.asarray(cols)


def _prep_layer(w_in, w_q_up, w_kv_up, w_out, w_router, w_gate, w_up, w_down):
    d = w_in.shape[0]
    perm = _pair_perm()
    o_ka = H_A * DH_A
    o_va = o_ka + KV_A * DH_A
    o_cq = o_va + KV_A * DH_A
    o_ckv = o_cq + Q_RANK
    o_kr = o_ckv + KV_RANK
    z = lambda w: jnp.zeros((d, w), F32)
    win = jnp.concatenate([
        w_in[:, :o_ka][:, perm], w_in[:, o_ka:o_va],
        w_in[:, o_va:o_va + DH_A], z(LANES - DH_A), w_in[:, o_va + DH_A:o_cq], z(LANES - DH_A),
        w_in[:, o_cq:o_ckv], w_in[:, o_ckv:o_kr],
        z(NOPE_B), w_in[:, o_kr:], z(LANES - NOPE_B - ROPE_B)], axis=1).astype(BF16)
    wq = w_q_up.reshape(Q_RANK, H_B, NOPE_B + ROPE_B)
    wq = jnp.pad(wq, ((0, 0), (0, 0), (0, LANES - NOPE_B - ROPE_B))).reshape(Q_RANK, H_B * LANES)
    wkv = w_kv_up.reshape(KV_RANK, H_B, NOPE_B + V_B)
    wk = jnp.pad(wkv[:, :, :NOPE_B], ((0, 0), (0, 0), (0, LANES - NOPE_B))).reshape(KV_RANK, H_B * LANES)
    wv = jnp.pad(wkv[:, :, NOPE_B:], ((0, 0), (0, 0), (0, LANES - V_B))).reshape(KV_RANK, H_B * LANES)
    wkv_p = jnp.concatenate([wk, wv], axis=1)
    wa = w_out[:H_A * DH_A][perm]
    wb = w_out[H_A * DH_A:]
    wr = jnp.pad(w_router, ((0, 0), (0, LANES - N_EXPERTS)))
    r_hi = wr.astype(BF16)
    r_lo = (wr - r_hi.astype(F32)).astype(BF16)
    wgu = jnp.concatenate([w_gate, w_up], axis=2).astype(BF16)
    return dict(win=win, wq=wq.astype(BF16), wkv=wkv_p.astype(BF16), wa=wa.astype(BF16),
                wb=wb.astype(BF16), r_hi=r_hi, r_lo=r_lo, wgu=wgu, wd=w_down.astype(BF16))


def _tile(n, pref):
    t = pref
    while n % t:
        t //= 2
    return t


def kernel(x, c, ctx, c_ctx, w_mod, b_mod, norm_attn, norm_ffn, w_in, sink, q_norm, kv_norm,
           w_q_up, w_kv_up, w_out, w_router, w_gate, w_up, w_down, norm_final):
    bsz, n, d = x.shape
    lc = ctx.shape[1]
    depth = w_mod.shape[0]
    assert bsz < SUBLANES and n % 256 == 0 and n >= 512 and lc % LANES == 0
    ctx_row = bsz

    cs = jnp.zeros((SUBLANES, d), F32).at[:bsz].set(c).at[ctx_row].set(c_ctx)
    mod = _mod_call(cs, w_mod, b_mod)
    tabs_lat = _rope_tables(n)
    tabs_ctx = _ident_tables(lc)

    tm = _tile(n, 512)
    tq_w = 256
    tq_m = _tile(n, 1024)
    tk_m = _tile(n, 2048)
    cap = CAP_FACTOR * n // N_EXPERTS
    cap_c = CAP_FACTOR * lc // N_EXPERTS
    ts = _tile(cap, 512)

    xl = x.reshape(bsz * n, d)
    xc = ctx.reshape(bsz * lc, d)
    row2 = lambda a: a.reshape(1, -1)

    for l in range(depth):
        last = l == depth - 1
        w = _prep_layer(w_in[l], w_q_up[l], w_kv_up[l], w_out[l], w_router[l],
                        w_gate[l], w_up[l], w_down[l])
        mod3 = mod[l].reshape(SUBLANES, 1, 6 * d)
        lat_row = lambda i: i // (n // tm)
        ctx_rowf = lambda i: ctx_row
        na, nf = row2(norm_attn[l]), row2(norm_ffn[l])
        qn, kvn = row2(q_norm[l]), row2(kv_norm[l])

        qa, ka, va, qb, kb, vb = _proj_call(xl, mod3, lat_row, na, w["win"], qn, kvn,
                                            w["wq"], w["wkv"], tabs_lat, tm, n // tm)
        qa_c, ka_c, va_c, qb_c, kb_c, vb_c = _proj_call(xc, mod3, ctx_rowf, na, w["win"], qn, kvn,
                                                        w["wq"], w["wkv"], tabs_ctx, lc, 1)
        o_a = _win_call(sink[l], qa, ka_c, va_c, ka, va, bsz, n, lc, tq_w)
        o_b = _mla_call(qb, kb_c, vb_c, kb, vb, bsz, n, lc, tq_m, tk_m)
        xl, h8, aff = _outp_call(o_a, o_b, xl, mod3, lat_row, w["wa"], w["wb"], nf,
                                 w["r_hi"], w["r_lo"], tm)
        idx3, val3 = _route_call(aff, bsz, n)
        moe8 = _moe_call(idx3, val3, h8, w["wgu"], w["wd"], ts, n)
        xl = _resid_call(xl, moe8, mod3, lat_row, row2(norm_final), tm, last)

        if not last:
            o_a_c = _win_call(sink[l], qa_c, ka_c, va_c, None, None, bsz, lc, lc, lc)
            o_b_c = _mla_call(qb_c, kb_c, vb_c, None, None, bsz, lc, lc, lc, lc)
            xc, h8c, aff_c = _outp_call(o_a_c, o_b_c, xc, mod3, ctx_rowf, w["wa"], w["wb"], nf,
                                        w["r_hi"], w["r_lo"], lc)
            idx3c, val3c = _route_call(aff_c, bsz, lc)
            moe8c = _moe_call(idx3c, val3c, h8c, w["wgu"], w["wd"], cap_c, lc)
            xc = _resid_call(xc, moe8c, mod3, ctx_rowf, row2(norm_final), lc, False)

    return xl.reshape(bsz, n, d)
```

```python
import functools
import math

import jax
import jax.numpy as jnp
import numpy as np
from jax import lax
from jax.experimental import pallas as pl
from jax.experimental.pallas import tpu as pltpu

F32 = jnp.float32
BF16 = jnp.bfloat16

GRID_W = 64
H_A, KV_A, DH_A = 8, 2, 64
G_A = H_A // KV_A
WINDOW = 128
H_B, NOPE_B, ROPE_B, V_B = 8, 64, 32, 64
Q_RANK, KV_RANK = 384, 256
N_EXPERTS, CAP_FACTOR, D_FF = 16, 2, 512
ROPE_BASE = 10000.0
EPS = 1e-6
LOG2E = math.log2(math.e)
LANES = 128
SUBLANES = 8
NEG_INF = float("-inf")

C_QA, C_KA, C_VA, C_CQ, C_CKV, C_KR, C_END = 0, 512, 640, 896, 1280, 1536, 1664
VMEM_LIMIT = 56 * 1024 * 1024
MOE_VMEM_LIMIT = 60 * 1024 * 1024


def _cparams(sem):
    return pltpu.CompilerParams(dimension_semantics=sem, vmem_limit_bytes=VMEM_LIMIT)


def _nt(a, b):
    return lax.dot_general(a, b, (((1,), (1,)), ((), ())), preferred_element_type=F32)


def _dot(a, b):
    return jnp.dot(a, b, preferred_element_type=F32)


def _split(a):
    hi = a.astype(BF16)
    lo = (a - hi.astype(F32)).astype(BF16)
    return hi, lo


def _dot3(a, b_hi, b_lo):
    a_hi, a_lo = _split(a)
    return _dot(a_hi, b_hi) + (_dot(a_hi, b_lo) + _dot(a_lo, b_hi))


def _lane_tile(t, reps):
    return t if reps == 1 else jnp.concatenate([t] * reps, axis=1)


def _rope(x, cos, sin, half):
    w = x.shape[1]
    reps = w // LANES
    lane = lax.broadcasted_iota(jnp.int32, x.shape, 1)
    first = (lane & (2 * half - 1)) < half
    partner = jnp.where(first, pltpu.roll(x, w - half, 1), pltpu.roll(x, half, 1))
    return x * _lane_tile(cos, reps) + partner * _lane_tile(sin, reps)


def _rms(x, w):
    ms = jnp.mean(x * x, axis=-1, keepdims=True)
    return x * lax.rsqrt(ms + EPS) * w


def _mod_body(c_ref, w_ref, b_ref, o_ref):
    c = c_ref[...]
    s = c * (1.0 / (1.0 + jnp.exp(-c)))
    w_hi, w_lo = _split(w_ref[0])
    o_ref[0] = _dot3(s, w_hi, w_lo) + b_ref[0]


def _mod_call(cs, w_mod, b_mod):
    depth, d, d6 = w_mod.shape
    tn = 1536
    return pl.pallas_call(
        _mod_body,
        grid=(depth, d6 // tn),
        in_specs=[
            pl.BlockSpec((SUBLANES, d), lambda l, j: (0, 0)),
            pl.BlockSpec((1, d, tn), lambda l, j: (l, 0, j)),
            pl.BlockSpec((1, 1, tn), lambda l, j: (l, 0, j)),
        ],
        out_specs=pl.BlockSpec((1, SUBLANES, tn), lambda l, j: (l, 0, j)),
        out_shape=jax.ShapeDtypeStruct((depth, SUBLANES, d6), F32),
        compiler_params=_cparams(("arbitrary", "arbitrary")),
        name="mod",
    )(cs, w_mod, b_mod.reshape(depth, 1, d6))


def _proj_body(x_ref, mod_ref, nw_ref, win_ref, qn_ref, kvn_ref, wq_ref, wkv_ref,
               caq, saq, cak, sak, cbq, sbq, cbk, sbk,
               qa_o, ka_o, va_o, qb_o, kb_o, vb_o):
    d = x_ref.shape[1]
    m = mod_ref[0]
    y = _rms(x_ref[...], nw_ref[...])
    h = (y * (1.0 + m[:, d:2 * d]) + m[:, 0:d]).astype(BF16)
    p = _dot(h, win_ref[...])
    qa_o[...] = _rope(p[:, C_QA:C_KA], caq[...], saq[...], 16).astype(BF16)
    ka_o[...] = _rope(p[:, C_KA:C_VA], cak[...], sak[...], 16).astype(BF16)
    alane = lax.broadcasted_iota(jnp.int32, (x_ref.shape[0], C_CQ - C_VA), 1)
    va_o[...] = (p[:, C_VA:C_CQ] + jnp.where((alane & (LANES - 1)) >= DH_A, 1.0, 0.0)).astype(BF16)
    cq = _rms(p[:, C_CQ:C_CKV], qn_ref[...]).astype(BF16)
    qb = _dot(cq, wq_ref[...])
    qb_o[...] = _rope(qb, cbq[...], sbq[...], 8).astype(BF16)
    ckv = _rms(p[:, C_CKV:C_KR], kvn_ref[...]).astype(BF16)
    kvb = _dot(ckv, wkv_ref[...])
    kr = _rope(p[:, C_KR:C_END], cbk[...], sbk[...], 8)
    kw = H_B * LANES
    kb_o[...] = (kvb[:, 0:kw] + _lane_tile(kr, H_B)).astype(BF16)
    vlane = lax.broadcasted_iota(jnp.int32, (x_ref.shape[0], kw), 1)
    ones_pad = jnp.where((vlane & (LANES - 1)) >= V_B, 1.0, 0.0)
    vb_o[...] = (kvb[:, kw:] + ones_pad).astype(BF16)


def _proj_call(x2, mod3, mod_row_fn, nw, win, qn, kvn, wq, wkv, tabs, tm, tab_blocks):
    t, d = x2.shape
    full = lambda a: pl.BlockSpec(a.shape, lambda i: (0,) * a.ndim)
    tab_spec = pl.BlockSpec((tm, LANES), lambda i: (i % tab_blocks, 0))
    widths = (512, 128, 256, 1024, 1024, 1024)
    return pl.pallas_call(
        _proj_body,
        grid=(t // tm,),
        in_specs=[
            pl.BlockSpec((tm, d), lambda i: (i, 0)),
            pl.BlockSpec((1, 1, mod3.shape[2]), lambda i: (mod_row_fn(i), 0, 0)),
            full(nw), full(win), full(qn), full(kvn), full(wq), full(wkv),
        ] + [tab_spec] * 8,
        out_specs=[pl.BlockSpec((tm, w), lambda i: (i, 0)) for w in widths],
        out_shape=[jax.ShapeDtypeStruct((t, w), BF16) for w in widths],
        compiler_params=_cparams(("arbitrary",)),
        name="proj",
    )(x2, mod3, nw, win, qn, kvn, wq, wkv, *tabs)


def _win_body(sink_ref, q_ref, kc_ref, vc_ref, *rest, n, tq, band):
    if band:
        k_ref, v_ref, o_ref = rest
    else:
        (o_ref,) = rest
    i = pl.program_id(1)
    kc = kc_ref[...]
    if band:
        wb = tq + 2 * WINDOW
        start = jnp.clip(i * tq - WINDOW, 0, n - wb)
        start = pl.multiple_of(start, LANES)
        kb = k_ref[pl.ds(start, wb), :]
        qpos = i * tq + lax.broadcasted_iota(jnp.int32, (tq, wb), 0)
        kpos = start + lax.broadcasted_iota(jnp.int32, (tq, wb), 1)
        valid = jnp.abs(qpos - kpos) <= WINDOW
    lo = lax.broadcasted_iota(jnp.int32, (tq, LANES), 1) < DH_A
    zero = jnp.zeros((tq, LANES), BF16)
    o_g = []
    for g in range(KV_A):
        sel = lo if g == 0 else jnp.logical_not(lo)
        qg = jnp.concatenate(
            [jnp.where(sel, q_ref[:, p * LANES:(p + 1) * LANES], zero) for p in range(G_A)], axis=0)
        gs = slice(g * LANES, (g + 1) * LANES)
        vc = vc_ref[:, gs]
        s_c = _nt(qg, kc)
        if band:
            s_b = _nt(qg, kb)
            vb = v_ref[pl.ds(start, wb), gs]
        o_p = []
        for p in range(G_A):
            h = p + g * G_A
            sink2 = sink_ref[h] * LOG2E
            sc = s_c[p * tq:(p + 1) * tq]
            m = jnp.maximum(jnp.max(sc, axis=-1, keepdims=True), sink2)
            if band:
                sb = jnp.where(valid, s_b[p * tq:(p + 1) * tq], NEG_INF)
                m = jnp.maximum(m, jnp.max(sb, axis=-1, keepdims=True))
            o = _dot(jnp.exp2((sc - m).astype(BF16)), vc)
            if band:
                o = o + _dot(jnp.exp2((sb - m).astype(BF16)), vb)
            den = pltpu.roll(o, DH_A, 1) + jnp.exp2(sink2 - m)
            o_p.append(o * (1.0 / den))
        o_g.append(o_p)
    for p in range(G_A):
        o_ref[:, p * LANES:(p + 1) * LANES] = jnp.where(
            lo, o_g[0][p], pltpu.roll(o_g[1][p], DH_A, 1)).astype(BF16)


def _win_call(sink, q, kc, vc, k, v, bsz, n, lc, tq):
    band = k is not None
    nq = n // tq
    in_specs = [
        pl.BlockSpec(memory_space=pltpu.SMEM),
        pl.BlockSpec((tq, H_A * DH_A), lambda b, i: (b * nq + i, 0)),
        pl.BlockSpec((lc, LANES), lambda b, i: (b, 0)),
        pl.BlockSpec((lc, KV_A * LANES), lambda b, i: (b, 0)),
    ]
    args = [sink, q, kc, vc]
    if band:
        in_specs += [pl.BlockSpec((n, LANES), lambda b, i: (b, 0)),
                     pl.BlockSpec((n, KV_A * LANES), lambda b, i: (b, 0))]
        args += [k, v]
    return pl.pallas_call(
        functools.partial(_win_body, n=n, tq=tq, band=band),
        grid=(bsz, nq),
        in_specs=in_specs,
        out_specs=pl.BlockSpec((tq, H_A * DH_A), lambda b, i: (b * nq + i, 0)),
        out_shape=jax.ShapeDtypeStruct((bsz * n, H_A * DH_A), BF16),
        compiler_params=_cparams(("arbitrary", "arbitrary")),
        name="win_lat" if band else "win_ctx",
    )(*args)


def _mla_body(q_ref, kc_ref, vc_ref, *rest, n, tk, has_lat):
    if has_lat:
        k_ref, v_ref, o_ref, m_sc, acc_sc = rest
    else:
        (o_ref,) = rest
    tq = q_ref.shape[0]
    lo = lax.broadcasted_iota(jnp.int32, (tq, LANES), 1) < V_B

    def lane_max(p):
        parts = [p[:, c * LANES:(c + 1) * LANES] for c in range(p.shape[1] // LANES)]
        while len(parts) > 1:
            parts = [jnp.maximum(a, b) for a, b in zip(parts[0::2], parts[1::2])] + (
                [parts[-1]] if len(parts) % 2 else [])
        return parts[0]

    def row_max_rep(p):
        r = jnp.max(lane_max(p), axis=-1, keepdims=True)
        return jnp.broadcast_to(r, (tq, LANES))

    def probs(s, m):
        return jnp.exp2((s - _lane_tile(m, s.shape[1] // LANES)).astype(BF16))

    qs, ms, accs = [], [], []
    for hh in range(2):
        hs = slice(hh * LANES, (hh + 1) * LANES)
        q = q_ref[:, hs]
        s = _nt(q, kc_ref[:, hs])
        m = row_max_rep(s)
        qs.append(q)
        ms.append(m)
        accs.append(_dot(probs(s, m), vc_ref[:, hs]))

    if has_lat:
        for hh in range(2):
            m_sc[hh] = ms[hh]
            acc_sc[hh] = accs[hh]

        def chunk(c, carry):
            off = pl.multiple_of(c * tk, tk)
            for hh in range(2):
                hs = slice(hh * LANES, (hh + 1) * LANES)
                s = _nt(qs[hh], k_ref[pl.ds(off, tk), hs])
                m_old = m_sc[hh]
                m_new = jnp.maximum(m_old, row_max_rep(s))
                alpha = jnp.exp2(m_old - m_new)
                m_sc[hh] = m_new
                acc_sc[hh] = alpha * acc_sc[hh] + _dot(probs(s, m_new), v_ref[pl.ds(off, tk), hs])
            return carry

        lax.fori_loop(0, n // tk, chunk, 0)
        accs = [acc_sc[0], acc_sc[1]]

    outs = [a * pltpu.roll(1.0 / a, V_B, 1) for a in accs]
    o_ref[...] = jnp.where(lo, outs[0], pltpu.roll(outs[1], V_B, 1)).astype(BF16)


def _mla_call(q, kc, vc, k, v, bsz, n, lc, tq, tk):
    has_lat = k is not None
    nq = n // tq
    npair = H_B // 2
    in_specs = [
        pl.BlockSpec((tq, 2 * LANES), lambda b, h, i: (b * nq + i, h)),
        pl.BlockSpec((lc, 2 * LANES), lambda b, h, i: (b, h)),
        pl.BlockSpec((lc, 2 * LANES), lambda b, h, i: (b, h)),
    ]
    args = [q, kc, vc]
    scratch = []
    if has_lat:
        in_specs += [pl.BlockSpec((n, 2 * LANES), lambda b, h, i: (b, h))] * 2
        args += [k, v]
        scratch = [pltpu.VMEM((2, tq, LANES), F32)] * 2
    return pl.pallas_call(
        functools.partial(_mla_body, n=n, tk=tk, has_lat=has_lat),
        grid=(bsz, npair, nq),
        in_specs=in_specs,
        out_specs=pl.BlockSpec((tq, LANES), lambda b, h, i: (b * nq + i, h)),
        out_shape=jax.ShapeDtypeStruct((bsz * n, H_B * V_B), BF16),
        scratch_shapes=scratch,
        compiler_params=_cparams(("arbitrary", "arbitrary", "arbitrary")),
        name="mla_lat" if has_lat else "mla_ctx",
    )(*args)


def _outp_body(oa_ref, ob_ref, x_ref, mod_ref, wa_ref, wb_ref, nw_ref, rhi_ref, rlo_ref,
               x_o, h_o, aff_o):
    d = x_ref.shape[1]
    tm = x_ref.shape[0]
    m = mod_ref[0]
    mix = _dot(oa_ref[...], wa_ref[...]) + _dot(ob_ref[...], wb_ref[...])
    x = x_ref[...] + m[:, 2 * d:3 * d] * mix
    x_o[...] = x
    h = _rms(x, nw_ref[...]) * (1.0 + m[:, 4 * d:5 * d]) + m[:, 3 * d:4 * d]
    for s in range(d // LANES):
        h_o[pl.ds(s, tm, stride=SUBLANES), :] = h[:, s * LANES:(s + 1) * LANES]
    logits = _dot3(h, rhi_ref[...], rlo_ref[...])
    lane = lax.broadcasted_iota(jnp.int32, logits.shape, 1)
    logits = jnp.where(lane < N_EXPERTS, logits, NEG_INF)
    e = jnp.exp(logits - jnp.max(logits, axis=-1, keepdims=True))
    aff_o[...] = e / jnp.sum(e, axis=-1, keepdims=True)


def _outp_call(oa, ob, x2, mod3, mod_row_fn, wa, wb, nw, rhi, rlo, tm):
    t, d = x2.shape
    full = lambda a: pl.BlockSpec(a.shape, lambda i: (0,) * a.ndim)
    row = lambda w: pl.BlockSpec((tm, w), lambda i: (i, 0))
    return pl.pallas_call(
        _outp_body,
        grid=(t // tm,),
        in_specs=[row(oa.shape[1]), row(ob.shape[1]), row(d),
                  pl.BlockSpec((1, 1, mod3.shape[2]), lambda i: (mod_row_fn(i), 0, 0)),
                  full(wa), full(wb), full(nw), full(rhi), full(rlo)],
        out_specs=[row(d), pl.BlockSpec((tm * SUBLANES, LANES), lambda i: (i, 0)), row(LANES)],
        out_shape=[jax.ShapeDtypeStruct((t, d), F32),
                   jax.ShapeDtypeStruct((t * SUBLANES, LANES), F32),
                   jax.ShapeDtypeStruct((t, LANES), F32)],
        compiler_params=_cparams(("arbitrary",)),
        name="outp",
    )(oa, ob, x2, mod3, wa, wb, nw, rhi, rlo)


def _route_body(a_ref, lblk_ref, idx_o, val_o, *, cap):
    n_exp, nblk = a_ref.shape[1], a_ref.shape[2]
    capl = idx_o.shape[2]
    rows = n_exp * nblk
    a3 = a_ref[0]
    a2 = a3.reshape(rows, LANES)

    ri = lax.broadcasted_iota(jnp.int32, (LANES, LANES), 0)
    ci = lax.broadcasted_iota(jnp.int32, (LANES, LANES), 1)
    upper = jnp.where(ri <= ci, 1.0, 0.0).astype(BF16)
    eye = jnp.where(ri == ci, 1.0, 0.0).astype(BF16)
    ones = jnp.ones((LANES, LANES), BF16)
    lblk = lblk_ref[...]

    def as01(mask3):
        return jnp.where(mask3, 1.0, 0.0).reshape(rows, LANES).astype(BF16)

    def red(x, op):
        return op(op(x, axis=1, keepdims=True), axis=2, keepdims=True)

    def count(mask3):
        return red(jnp.where(mask3, 1.0, 0.0), jnp.sum)

    def prefix(m01):
        within = _dot(m01, upper)
        tot = _dot(m01, ones)
        off = _dot(lblk, tot.astype(BF16))
        return within, tot, off

    real = a3 >= 0.0
    big = jnp.float32(3.0e38)

    def undecided(state):
        it, lo, hi = state
        return jnp.logical_and(it < 2048, jnp.sum(jnp.where(lo < hi, 1.0, 0.0)) > 0.0)

    def bisect(state):
        it, lo, hi = state
        mid = lo + (hi - lo) * 0.5
        mid = jnp.where(mid >= hi, lo, mid)
        above = a3 > mid
        up = count(above) >= cap
        lo_up = red(jnp.where(above, a3, big), jnp.min)
        hi_dn = red(jnp.where(jnp.logical_or(above, jnp.logical_not(real)), -big, a3), jnp.max)
        return it + 1, jnp.where(up, lo_up, lo), jnp.where(up, hi, hi_dn)

    lo0 = red(jnp.where(real, a3, big), jnp.min)
    hi0 = red(a3, jnp.max)
    _, thr, _ = lax.while_loop(undecided, bisect, (jnp.int32(0), lo0, hi0))

    gt = a3 > thr
    eq = a3 == thr
    need = cap - count(gt)
    eq01 = as01(eq)
    w_eq, _, off_eq = prefix(eq01)
    rank_eq = (off_eq + w_eq - eq01.astype(F32)).reshape(n_exp, nblk, LANES)
    sel01 = as01(jnp.logical_or(gt, jnp.logical_and(eq, rank_eq < need)))
    within, tot, off = prefix(sel01)
    wsel = within * sel01.astype(F32)

    j = lax.broadcasted_iota(jnp.int32, (nblk, capl), 1).astype(F32)
    cbase = (lax.broadcasted_iota(jnp.int32, (nblk, capl), 0) * LANES).astype(F32)
    lidx = lax.broadcasted_iota(jnp.int32, (LANES, capl), 0).astype(F32)
    reps = capl // LANES
    for e in range(n_exp):
        rs = slice(e * nblk, (e + 1) * nblk)
        off_e = _lane_tile(off[rs], reps)
        in_blk = jnp.logical_and(j >= off_e, j < off_e + _lane_tile(tot[rs], reps))
        at = jnp.where(in_blk, 1.0, 0.0).astype(BF16)
        rj = jnp.sum(jnp.where(in_blk, j - off_e + 1.0, 0.0), axis=0, keepdims=True)
        base = jnp.sum(jnp.where(in_blk, cbase, 0.0), axis=0, keepdims=True)
        ranks = _dot(_nt(eye, wsel[rs].astype(BF16)).astype(BF16), at)
        hit = jnp.logical_and(ranks == rj, rj > 0.0)
        local = jnp.sum(jnp.where(hit, lidx, 0.0), axis=0, keepdims=True)
        idx_o[0, e:e + 1, :] = (base + local).astype(jnp.int32)
        a_e = a2[rs]
        hi = a_e.astype(BF16)
        r1 = a_e - hi.astype(F32)
        mid = r1.astype(BF16)
        lo = (r1 - mid.astype(F32)).astype(BF16)
        affs = sum(_dot(_nt(eye, part).astype(BF16), at) for part in (hi, mid, lo))
        val_o[0, e:e + 1, :] = jnp.sum(jnp.where(hit, affs, 0.0), axis=0, keepdims=True)


def _route_call(aff, bsz, n):
    cap = CAP_FACTOR * n // N_EXPERTS
    capl = -(-cap // LANES) * LANES
    nblk = max(n // LANES, 16)
    a = jnp.swapaxes(aff[:, :N_EXPERTS].reshape(bsz, n, N_EXPERTS), 1, 2)
    a = jnp.pad(a, ((0, 0), (0, 0), (0, nblk * LANES - n)), constant_values=-1.0)
    a = a.reshape(bsz, N_EXPERTS, nblk, LANES)
    rows = N_EXPERTS * nblk
    r = np.arange(rows)
    lblk = jnp.asarray((r[:, None] // nblk == r[None, :] // nblk) & (r[None, :] < r[:, None]), BF16)
    idx, val = pl.pallas_call(
        functools.partial(_route_body, cap=cap),
        grid=(bsz,),
        in_specs=[pl.BlockSpec((1, N_EXPERTS, nblk, LANES), lambda b: (b, 0, 0, 0)),
                  pl.BlockSpec((rows, rows), lambda b: (0, 0))],
        out_specs=[pl.BlockSpec((1, N_EXPERTS, capl), lambda b: (b, 0, 0))] * 2,
        out_shape=[jax.ShapeDtypeStruct((bsz, N_EXPERTS, capl), jnp.int32),
                   jax.ShapeDtypeStruct((bsz, N_EXPERTS, capl), F32)],
        compiler_params=_cparams(("arbitrary",)),
        name="route",
    )(a, lblk)
    return (idx[:, :, :cap].reshape(bsz * N_EXPERTS, 1, cap),
            val[:, :, :cap].reshape(bsz * N_EXPERTS, 1, cap))


def _moe_body(row_cur, nxt_ref, val_ref, h_hbm, wgu_ref, wd_ref, out_hbm,
              buf0, buf1, ybuf, acc, sem0, sem1, osem, *, ts, nt, n_tok, n_exp):
    be = pl.program_id(0)
    t = pl.program_id(1)
    b = be // n_exp
    e = be % n_exp
    step = be * nt + t
    total = pl.num_programs(0) * nt
    rows = ts * SUBLANES

    def whole(buf_ref, sem_ref):
        return pltpu.make_async_copy(h_hbm.at[pl.ds(0, rows), :], buf_ref, sem_ref)

    @pl.when(step == 0)
    def _():
        def grp(g, carry):
            for s in range(SUBLANES):
                j = g * SUBLANES + s
                r = pl.multiple_of(row_cur[0, 0, j] + b * (n_tok * SUBLANES), SUBLANES)
                pltpu.make_async_copy(
                    h_hbm.at[pl.ds(r, SUBLANES), :],
                    buf0.at[pl.ds(pl.multiple_of(j * SUBLANES, SUBLANES), SUBLANES), :], sem0).start()
            return carry
        lax.fori_loop(0, ts // SUBLANES, grp, 0)

    @pl.when(jnp.logical_and(e == 0, t == 0))
    def _():
        zrows = min(n_tok * SUBLANES, 4096)

        def zero(i, carry):
            acc[pl.ds(pl.multiple_of(i * zrows, zrows), zrows), :] = jnp.zeros((zrows, LANES), F32)
            return carry
        lax.fori_loop(0, n_tok * SUBLANES // zrows, zero, 0)

    def run(src, ssem, dst, dsem):
        whole(src, ssem).wait()
        for j in range(ts):
            pltpu.make_async_copy(
                h_hbm.at[pl.ds(pl.multiple_of(nxt_ref[0, 0, j], SUBLANES), SUBLANES), :],
                dst.at[pl.ds(j * SUBLANES, SUBLANES), :], dsem).start(priority=j % 2)

        d = wgu_ref.shape[1]
        x = jnp.concatenate(
            [src[pl.ds(s, ts, stride=SUBLANES), :] for s in range(d // LANES)], axis=1).astype(BF16)
        au = _dot(x, wgu_ref[0])
        dff = au.shape[1] // 2
        a = au[:, :dff]
        hmid = (a * (1.0 / (1.0 + jnp.exp(-a))) * au[:, dff:]).astype(BF16)
        y = _dot(hmid, wd_ref[0])
        for c in range(d // LANES):
            ybuf[:, c * SUBLANES:(c + 1) * SUBLANES, :] = y[:, c * LANES:(c + 1) * LANES].reshape(
                ts // SUBLANES, SUBLANES, LANES)

        def scatter(g, carry):
            upd = []
            for s in range(SUBLANES):
                j = t * ts + g * SUBLANES + s
                r = pl.multiple_of(row_cur[0, 0, j], SUBLANES)
                yrow = ybuf[g, pl.ds(s, SUBLANES, stride=SUBLANES), :]
                upd.append((r, acc[pl.ds(r, SUBLANES), :] + yrow * val_ref[0, 0, j]))
            for r, v in upd:
                acc[pl.ds(r, SUBLANES), :] = v
            return carry
        lax.fori_loop(0, ts // SUBLANES, scatter, 0)

        @pl.when(step == total - 1)
        def _():
            whole(dst, dsem).wait()

    @pl.when(step % 2 == 0)
    def _():
        run(buf0, sem0, buf1, sem1)

    @pl.when(step % 2 == 1)
    def _():
        run(buf1, sem1, buf0, sem0)

    @pl.when(jnp.logical_and(e == n_exp - 1, t == nt - 1))
    def _():
        nrows = n_tok * SUBLANES
        flush = pltpu.make_async_copy(
            acc, out_hbm.at[pl.ds(pl.multiple_of(b * nrows, nrows), nrows), :], osem)
        flush.start()
        flush.wait()


def _moe_call(idx3, val3, h8, wgu, wd, ts, n_tok):
    nbe, _, cap = idx3.shape
    nt = cap // ts
    e = wgu.shape[0]
    tok0 = (jnp.arange(nbe, dtype=jnp.int32) // e * n_tok)[:, None, None]
    nxt = jnp.roll(((idx3 + tok0) * SUBLANES).reshape(nbe * nt, 1, ts), -1, axis=0)
    smem = lambda imap: pl.BlockSpec((1, 1, cap), imap, memory_space=pltpu.SMEM)
    return pl.pallas_call(
        functools.partial(_moe_body, ts=ts, nt=nt, n_tok=n_tok, n_exp=e),
        grid=(nbe, nt),
        in_specs=[
            smem(lambda be, t: (be, 0, 0)),
            pl.BlockSpec((1, 1, ts), lambda be, t: (be * nt + t, 0, 0), memory_space=pltpu.SMEM),
            smem(lambda be, t: (be, 0, 0)),
            pl.BlockSpec(memory_space=pl.ANY),
            pl.BlockSpec((1,) + wgu.shape[1:], lambda be, t: (be % e, 0, 0)),
            pl.BlockSpec((1,) + wd.shape[1:], lambda be, t: (be % e, 0, 0)),
        ],
        out_specs=pl.BlockSpec(memory_space=pl.ANY),
        out_shape=jax.ShapeDtypeStruct(h8.shape, F32),
        scratch_shapes=[
            pltpu.VMEM((ts * SUBLANES, LANES), F32),
            pltpu.VMEM((ts * SUBLANES, LANES), F32),
            pltpu.VMEM((ts // SUBLANES, wgu.shape[1] // LANES * SUBLANES, LANES), F32),
            pltpu.VMEM((n_tok * SUBLANES, LANES), F32),
            pltpu.SemaphoreType.DMA(()),
            pltpu.SemaphoreType.DMA(()),
            pltpu.SemaphoreType.DMA(()),
        ],
        compiler_params=pltpu.CompilerParams(dimension_semantics=("arbitrary", "arbitrary"),
                                             vmem_limit_bytes=MOE_VMEM_LIMIT),
        name="experts",
    )(idx3 * SUBLANES, nxt, val3, h8, wgu, wd)


def _resid_body(x_ref, y_ref, mod_ref, nw_ref, o_ref, *, final):
    tm, d = x_ref.shape
    y = jnp.concatenate(
        [y_ref[pl.ds(s, tm, stride=SUBLANES), :] for s in range(d // LANES)], axis=1)
    x = x_ref[...] + mod_ref[0][:, 5 * d:6 * d] * y
    o_ref[...] = _rms(x, nw_ref[...]) if final else x


def _resid_call(x2, y8, mod3, mod_row_fn, nw, tm, final):
    t, d = x2.shape
    row = pl.BlockSpec((tm, d), lambda i: (i, 0))
    return pl.pallas_call(
        functools.partial(_resid_body, final=final),
        grid=(t // tm,),
        in_specs=[row, pl.BlockSpec((tm * SUBLANES, LANES), lambda i: (i, 0)),
                  pl.BlockSpec((1, 1, mod3.shape[2]), lambda i: (mod_row_fn(i), 0, 0)),
                  pl.BlockSpec(nw.shape, lambda i: (0, 0))],
        out_specs=row,
        out_shape=jax.ShapeDtypeStruct((t, d), F32),
        compiler_params=_cparams(("arbitrary",)),
        name="resid",
    )(x2, y8, mod3, nw)


def _rope_tables(n):
    pos_r = (np.arange(n) // GRID_W).astype(np.float32)
    pos_c = (np.arange(n) % GRID_W).astype(np.float32)

    def pattern(nf):
        inv = ROPE_BASE ** (-np.arange(nf, dtype=np.float32) / nf)
        ar = pos_r[:, None] * inv[None, :]
        ac = pos_c[:, None] * inv[None, :]
        cos = np.concatenate([np.cos(ar), np.cos(ar), np.cos(ac), np.cos(ac)], axis=1)
        sin = np.concatenate([-np.sin(ar), np.sin(ar), -np.sin(ac), np.sin(ac)], axis=1)
        return cos.astype(np.float32), sin.astype(np.float32)

    ca, sa = pattern(DH_A // 4)
    cb, sb = pattern(ROPE_B // 4)
    ca2, sa2 = np.tile(ca, (1, 2)), np.tile(sa, (1, 2))
    one64 = np.ones((n, NOPE_B), np.float32)
    z64 = np.zeros((n, NOPE_B), np.float32)
    z32 = np.zeros((n, LANES - NOPE_B - ROPE_B), np.float32)
    cbp = np.concatenate([one64, cb, z32], axis=1)
    sbp = np.concatenate([z64, sb, z32], axis=1)
    sc_a = DH_A ** -0.5 * LOG2E
    sc_b = (NOPE_B + ROPE_B) ** -0.5 * LOG2E
    tabs = [ca2 * sc_a, sa2 * sc_a, ca2, sa2, cbp * sc_b, sbp * sc_b, cbp, sbp]
    return [jnp.asarray(t, F32) for t in tabs]


def _ident_tables(n):
    one = np.ones((n, LANES), np.float32)
    zero = np.zeros((n, LANES), np.float32)
    z32 = np.zeros((n, LANES - NOPE_B - ROPE_B), np.float32)
    onep = np.concatenate([np.ones((n, NOPE_B + ROPE_B), np.float32), z32], axis=1)
    sc_a = DH_A ** -0.5 * LOG2E
    sc_b = (NOPE_B + ROPE_B) ** -0.5 * LOG2E
    tabs = [one * sc_a, zero, one, zero, onep * sc_b, zero, onep, zero]
    return [jnp.asarray(t, F32) for t in tabs]


def _pair_perm():
    cols = []
    for p in range(G_A):
        for h in (p, G_A + p):
            cols.extend(range(h * DH_A, (h + 1) * DH_A))
    return np.asarray(cols)


def _prep_layer(w_in, w_q_up, w_kv_up, w_out, w_router, w_gate, w_up, w_down):
    d = w_in.shape[0]
    perm = _pair_perm()
    o_ka = H_A * DH_A
    o_va = o_ka + KV_A * DH_A
    o_cq = o_va + KV_A * DH_A
    o_ckv = o_cq + Q_RANK
    o_kr = o_ckv + KV_RANK
    z = lambda w: jnp.zeros((d, w), F32)
    win = jnp.concatenate([
        w_in[:, :o_ka][:, perm], w_in[:, o_ka:o_va],
        w_in[:, o_va:o_va + DH_A], z(LANES - DH_A), w_in[:, o_va + DH_A:o_cq], z(LANES - DH_A),
        w_in[:, o_cq:o_ckv], w_in[:, o_ckv:o_kr],
        z(NOPE_B), w_in[:, o_kr:], z(LANES - NOPE_B - ROPE_B)], axis=1).astype(BF16)
    wq = w_q_up.reshape(Q_RANK, H_B, NOPE_B + ROPE_B)
    wq = jnp.pad(wq, ((0, 0), (0, 0), (0, LANES - NOPE_B - ROPE_B))).reshape(Q_RANK, H_B * LANES)
    wkv = w_kv_up.reshape(KV_RANK, H_B, NOPE_B + V_B)
    wk = jnp.pad(wkv[:, :, :NOPE_B], ((0, 0), (0, 0), (0, LANES - NOPE_B))).reshape(KV_RANK, H_B * LANES)
    wv = jnp.pad(wkv[:, :, NOPE_B:], ((0, 0), (0, 0), (0, LANES - V_B))).reshape(KV_RANK, H_B * LANES)
    wkv_p = jnp.concatenate([wk, wv], axis=1)
    wa = w_out[:H_A * DH_A][perm]
    wb = w_out[H_A * DH_A:]
    wr = jnp.pad(w_router, ((0, 0), (0, LANES - N_EXPERTS)))
    r_hi = wr.astype(BF16)
    r_lo = (wr - r_hi.astype(F32)).astype(BF16)
    wgu = jnp.concatenate([w_gate, w_up], axis=2).astype(BF16)
    return dict(win=win, wq=wq.astype(BF16), wkv=wkv_p.astype(BF16), wa=wa.astype(BF16),
                wb=wb.astype(BF16), r_hi=r_hi, r_lo=r_lo, wgu=wgu, wd=w_down.astype(BF16))


def _tile(n, pref):
    t = pref
    while n % t:
        t //= 2
    return t


def kernel(x, c, ctx, c_ctx, w_mod, b_mod, norm_attn, norm_ffn, w_in, sink, q_norm, kv_norm,
           w_q_up, w_kv_up, w_out, w_router, w_gate, w_up, w_down, norm_final):
    bsz, n, d = x.shape
    lc = ctx.shape[1]
    depth = w_mod.shape[0]
    assert bsz < SUBLANES and n % 256 == 0 and n >= 512 and lc % LANES == 0
    ctx_row = bsz

    cs = jnp.zeros((SUBLANES, d), F32).at[:bsz].set(c).at[ctx_row].set(c_ctx)
    mod = _mod_call(cs, w_mod, b_mod)
    tabs_lat = _rope_tables(n)
    tabs_ctx = _ident_tables(lc)

    tm = _tile(n, 512)
    tq_w = 256
    tq_m = _tile(n, 1024)
    tk_m = _tile(n, 2048)
    cap = CAP_FACTOR * n // N_EXPERTS
    cap_c = CAP_FACTOR * lc // N_EXPERTS
    ts = _tile(cap, 512)

    xl = x.reshape(bsz * n, d)
    xc = ctx.reshape(bsz * lc, d)
    row2 = lambda a: a.reshape(1, -1)

    for l in range(depth):
        last = l == depth - 1
        w = _prep_layer(w_in[l], w_q_up[l], w_kv_up[l], w_out[l], w_router[l],
                        w_gate[l], w_up[l], w_down[l])
        mod3 = mod[l].reshape(SUBLANES, 1, 6 * d)
        lat_row = lambda i: i // (n // tm)
        ctx_rowf = lambda i: ctx_row
        na, nf = row2(norm_attn[l]), row2(norm_ffn[l])
        qn, kvn = row2(q_norm[l]), row2(kv_norm[l])

        qa, ka, va, qb, kb, vb = _proj_call(xl, mod3, lat_row, na, w["win"], qn, kvn,
                                            w["wq"], w["wkv"], tabs_lat, tm, n // tm)
        qa_c, ka_c, va_c, qb_c, kb_c, vb_c = _proj_call(xc, mod3, ctx_rowf, na, w["win"], qn, kvn,
                                                        w["wq"], w["wkv"], tabs_ctx, lc, 1)
        o_a = _win_call(sink[l], qa, ka_c, va_c, ka, va, bsz, n, lc, tq_w)
        o_b = _mla_call(qb, kb_c, vb_c, kb, vb, bsz, n, lc, tq_m, tk_m)
        xl, h8, aff = _outp_call(o_a, o_b, xl, mod3, lat_row, w["wa"], w["wb"], nf,
                                 w["r_hi"], w["r_lo"], tm)
        idx3, val3 = _route_call(aff, bsz, n)
        moe8 = _moe_call(idx3, val3, h8, w["wgu"], w["wd"], ts, n)
        xl = _resid_call(xl, moe8, mod3, lat_row, row2(norm_final), tm, last)

        if not last:
            o_a_c = _win_call(sink[l], qa_c, ka_c, va_c, None, None, bsz, lc, lc, lc)
            o_b_c = _mla_call(qb_c, kb_c, vb_c, None, None, bsz, lc, lc, lc, lc)
            xc, h8c, aff_c = _outp_call(o_a_c, o_b_c, xc, mod3, ctx_rowf, w["wa"], w["wb"], nf,
                                        w["r_hi"], w["r_lo"], lc)
            idx3c, val3c = _route_call(aff_c, bsz, lc)
            moe8c = _moe_call(idx3c, val3c, h8c, w["wgu"], w["wd"], cap_c, lc)
            xc = _resid_call(xc, moe8c, mod3, ctx_rowf, row2(norm_final), lc, False)

    return xl.reshape(bsz, n, d)
```
